```python
import jax, jax.numpy as jnp
from jax import lax
import numpy as np

D_MODEL = 1024
BATCH = 2
SEQ = 8192
DEPTH = 1

D_MIX = D_MODEL
CONV_WIDTH = D_MIX // 2
CONV_K = 3
RWKV_WIDTH = D_MIX - CONV_WIDTH
RWKV_HEAD = 64
RWKV_HEADS = RWKV_WIDTH // RWKV_HEAD
LORA_DECAY = 64
LORA_ICLR = 64
LORA_GATE = 128
RWKV_COLS = 3 * RWKV_WIDTH + LORA_DECAY + LORA_ICLR + LORA_GATE
IN_COLS = 3 * CONV_WIDTH + RWKV_COLS
N_EXPERTS = 32
TOP_K = 4
D_FF = D_MODEL
SWIGLU_LIMIT = 7.0
SWIGLU_ALPHA = 1.702
EXPERT_BLOCK = 128
RMS_EPS = 1e-5
GN_EPS = 64e-5

kernel_name = "hybrid_shortconv_rwkv7_moe_block"


def rms_norm(x, w):
    xf = x.astype(jnp.float32)
    y = xf * lax.rsqrt(jnp.mean(xf * xf, axis=-1, keepdims=True) + RMS_EPS)
    return (y * w.astype(jnp.float32)).astype(x.dtype)


def token_shift(p, mu):
    prev = jnp.pad(p, ((0, 0), (1, 0), (0, 0)))[:, :-1]
    return p + (prev - p) * mu


def causal_dwconv(u, w):
    S = u.shape[1]
    up = jnp.pad(u, ((0, 0), (CONV_K - 1, 0), (0, 0)))
    y = w[0] * up[:, 0:S]
    for j in range(1, CONV_K):
        y = y + w[j] * up[:, j:j + S]
    return y


def rwkv7_scan(r, w, k, v, a, b):
    Bsz, S, H, N = r.shape

    def step(state, inp):
        r_t, w_t, k_t, v_t, a_t, b_t = inp
        sa = jnp.einsum('bhij,bhj->bhi', state, a_t)
        state = (state * w_t[:, :, None, :]
                 + sa[..., :, None] * b_t[..., None, :]
                 + v_t[..., :, None] * k_t[..., None, :])
        y_t = jnp.einsum('bhij,bhj->bhi', state, r_t)
        return state, y_t

    xs = (jnp.moveaxis(r, 1, 0), jnp.moveaxis(w, 1, 0), jnp.moveaxis(k, 1, 0),
          jnp.moveaxis(v, 1, 0), jnp.moveaxis(a, 1, 0), jnp.moveaxis(b, 1, 0))
    s0 = jnp.zeros((Bsz, H, N, N), jnp.float32)
    _, ys = lax.scan(step, s0, xs)
    return jnp.moveaxis(ys, 0, 1)


def hybrid_mixer(xn, w_in, conv_w, mu_shift, w0, w_decay_up, a0, w_iclr_up, w_gate_up,
                 k_k, k_a, r_k, ln_x_w, ln_x_b, w_out):
    f32 = jnp.float32
    Bsz, S, _ = xn.shape
    H, N = RWKV_HEADS, RWKV_HEAD
    p = xn @ w_in
    c_b, c_c, c_h, p_rwkv = jnp.split(p, [CONV_WIDTH, 2 * CONV_WIDTH, 3 * CONV_WIDTH], axis=-1)

    y_conv = c_b * causal_dwconv(c_c * c_h, conv_w)

    q = token_shift(p_rwkv, mu_shift)
    o1 = 3 * RWKV_WIDTH
    r, k, v, d_w, d_a, d_g = jnp.split(
        q, [RWKV_WIDTH, 2 * RWKV_WIDTH, o1, o1 + LORA_DECAY, o1 + LORA_DECAY + LORA_ICLR], axis=-1)
    w_log = -jax.nn.softplus(-(w0 + jnp.tanh(d_w) @ w_decay_up).astype(f32)) - 0.5
    decay = jnp.exp(-jnp.exp(w_log))
    iclr = jax.nn.sigmoid((a0 + d_a @ w_iclr_up).astype(f32))
    g = jax.nn.sigmoid(d_g) @ w_gate_up

    def heads(t):
        return t.astype(f32).reshape(Bsz, S, H, N)

    r, k, v, decay, iclr = heads(r), heads(k), heads(v), heads(decay), heads(iclr)
    kk = k * k_k.astype(f32).reshape(H, N)
    kk = kk / jnp.maximum(jnp.linalg.norm(kk, axis=-1, keepdims=True), 1e-12)
    k = k * (1.0 + (iclr - 1.0) * k_a.astype(f32).reshape(H, N))
    y = rwkv7_scan(r, decay, k, v, -kk, kk * iclr)
    mean = jnp.mean(y, axis=-1, keepdims=True)
    var = jnp.mean(jnp.square(y - mean), axis=-1, keepdims=True)
    y = ((y - mean) * lax.rsqrt(var + GN_EPS) * ln_x_w.astype(f32).reshape(H, N)
         + ln_x_b.astype(f32).reshape(H, N))
    y = y + jnp.sum(r * k * r_k.astype(f32), axis=-1, keepdims=True) * v
    y_rwkv = y.reshape(Bsz, S, RWKV_WIDTH).astype(xn.dtype) * g

    return jnp.concatenate([y_conv, y_rwkv], axis=-1) @ w_out


def moe_ffn(xn, router_w, router_b, gu_w, gu_b, dn_w, dn_b):
    f32 = jnp.float32
    Bsz, S, D = xn.shape
    T = Bsz * S
    xt = xn.reshape(T, D)
    logits = xt.astype(f32) @ router_w.astype(f32) + router_b.astype(f32)
    top_vals, top_idx = lax.top_k(logits, TOP_K)
    gates = jax.nn.softmax(top_vals, axis=-1)

    n_pairs = T * TOP_K
    flat_e = top_idx.reshape(-1).astype(jnp.int32)
    flat_tok = jnp.arange(n_pairs, dtype=jnp.int32) // TOP_K
    flat_g = gates.reshape(-1)
    order = jnp.argsort(flat_e)
    se = flat_e[order]
    counts = jnp.bincount(flat_e, length=N_EXPERTS)
    starts = jnp.cumsum(counts) - counts
    pcounts = (counts + EXPERT_BLOCK - 1) // EXPERT_BLOCK * EXPERT_BLOCK
    pends = jnp.cumsum(pcounts)
    pstarts = pends - pcounts
    dest = pstarts[se] + (jnp.arange(n_pairs, dtype=jnp.int32) - starts[se])
    n_rows = -(-n_pairs // EXPERT_BLOCK) * EXPERT_BLOCK + N_EXPERTS * EXPERT_BLOCK
    n_blocks = n_rows // EXPERT_BLOCK
    row_tok = jnp.zeros((n_rows,), jnp.int32).at[dest].set(flat_tok[order])
    row_gate = jnp.zeros((n_rows,), f32).at[dest].set(flat_g[order])
    blk_e = jnp.minimum(
        jnp.searchsorted(pends, jnp.arange(n_blocks, dtype=jnp.int32) * EXPERT_BLOCK, side='right'),
        N_EXPERTS - 1).astype(jnp.int32)
    xs = xt[row_tok].reshape(n_blocks, EXPERT_BLOCK, D)

    def expert_block(args):
        xb, e = args
        h = xb @ gu_w[e] + gu_b[e]
        x_glu = jnp.minimum(h[:, 0::2], SWIGLU_LIMIT)
        x_lin = jnp.clip(h[:, 1::2], -SWIGLU_LIMIT, SWIGLU_LIMIT)
        act = x_glu * jax.nn.sigmoid(SWIGLU_ALPHA * x_glu) * (x_lin + 1.0)
        return act @ dn_w[e] + dn_b[e]

    yb = lax.map(expert_block, (xs, blk_e)).reshape(n_rows, D)
    contrib = (yb.astype(f32) * row_gate[:, None]).astype(xt.dtype)
    y = jnp.zeros((T, D), xt.dtype).at[row_tok].add(contrib)
    return y.reshape(Bsz, S, D)


def setup_inputs(seed: int = 0) -> dict:
    key = jax.random.key(seed)
    ks = jax.random.split(key, 26)
    f32 = jnp.float32
    L, D, H, N = DEPTH, D_MODEL, RWKV_HEADS, RWKV_HEAD

    def nrm(k, shape, scale):
        return jax.random.normal(k, shape, f32) * scale

    return {
        "x": nrm(ks[0], (BATCH, SEQ, D), 1.0),
        "norm_mix_w": 1.0 + nrm(ks[1], (L, D), 0.02),
        "w_in": nrm(ks[2], (L, D, IN_COLS), D ** -0.5),
        "conv_w": nrm(ks[3], (L, CONV_K, CONV_WIDTH), CONV_K ** -0.5),
        "mu_shift": jax.random.uniform(ks[4], (L, RWKV_COLS), f32, 0.0, 1.0),
        "w0": jax.random.uniform(ks[5], (L, RWKV_WIDTH), f32, -6.0, 1.0),
        "w_decay_up": nrm(ks[6], (L, LORA_DECAY, RWKV_WIDTH), 0.1),
        "a0": nrm(ks[7], (L, RWKV_WIDTH), 0.3),
        "w_iclr_up": nrm(ks[8], (L, LORA_ICLR, RWKV_WIDTH), 0.1),
        "w_gate_up": nrm(ks[9], (L, LORA_GATE, RWKV_WIDTH), LORA_GATE ** -0.5),
        "k_k": 0.85 + nrm(ks[10], (L, RWKV_WIDTH), 0.05),
        "k_a": 1.0 + nrm(ks[11], (L, RWKV_WIDTH), 0.05),
        "r_k": nrm(ks[12], (L, H, N), 0.1),
        "ln_x_w": 1.0 + nrm(ks[13], (L, RWKV_WIDTH), 0.02),
        "ln_x_b": nrm(ks[14], (L, RWKV_WIDTH), 0.02),
        "w_out": nrm(ks[15], (L, D_MIX, D), D_MIX ** -0.5),
        "norm_ffn_w": 1.0 + nrm(ks[16], (L, D), 0.02),
        "router_w": nrm(ks[17], (L, D, N_EXPERTS), D ** -0.5),
        "router_b": nrm(ks[18], (L, N_EXPERTS), 0.01),
        "gu_w": nrm(ks[19], (L, N_EXPERTS, D, 2 * D_FF), D ** -0.5),
        "gu_b": nrm(ks[20], (L, N_EXPERTS, 2 * D_FF), 0.01),
        "dn_w": nrm(ks[21], (L, N_EXPERTS, D_FF, D), D_FF ** -0.5),
        "dn_b": nrm(ks[22], (L, N_EXPERTS, D), 0.01),
        "norm_final_w": 1.0 + nrm(ks[23], (D,), 0.02),
    }


def reference(x, norm_mix_w, w_in, conv_w, mu_shift, w0, w_decay_up, a0, w_iclr_up, w_gate_up,
              k_k, k_a, r_k, ln_x_w, ln_x_b, w_out, norm_ffn_w, router_w, router_b,
              gu_w, gu_b, dn_w, dn_b, norm_final_w):
    h = x
    for l in range(DEPTH):
        h = h + hybrid_mixer(rms_norm(h, norm_mix_w[l]), w_in[l], conv_w[l], mu_shift[l], w0[l],
                             w_decay_up[l], a0[l], w_iclr_up[l], w_gate_up[l], k_k[l], k_a[l],
                             r_k[l], ln_x_w[l], ln_x_b[l], w_out[l])
        h = h + moe_ffn(rms_norm(h, norm_ffn_w[l]), router_w[l], router_b[l],
                        gu_w[l], gu_b[l], dn_w[l], dn_b[l])
    return rms_norm(h, norm_final_w)
```

```python
import functools

import jax
import jax.numpy as jnp
from jax import lax
from jax.experimental import pallas as pl
from jax.experimental.pallas import tpu as pltpu

F32 = jnp.float32
BF16 = jnp.bfloat16
I32 = jnp.int32

CONV_WIDTH = 512
CONV_K = 3
RWKV_WIDTH = 512
HEAD = 64
HEADS = RWKV_WIDTH // HEAD
LORA_DECAY = 64
LORA_ICLR = 64
LORA_GATE = 128
N_EXPERTS = 32
TOP_K = 4
SWIGLU_LIMIT = 7.0
SWIGLU_ALPHA = 1.702
RMS_EPS = 1e-5
GN_EPS = 64e-5

CHUNK = 64
SUB = 16
ROW_BLOCK = 256
VMEM_LIMIT = 56 * 1024 * 1024


def _params(sem, vmem=VMEM_LIMIT):
    return pltpu.CompilerParams(dimension_semantics=sem, vmem_limit_bytes=vmem)


def _mm(a, b):
    return jnp.dot(a.astype(BF16), b.astype(BF16), preferred_element_type=F32)


def _mm_nt(a, b):
    return lax.dot_general(a.astype(BF16), b.astype(BF16), (((1,), (1,)), ((), ())),
                           preferred_element_type=F32)


def _mm_tn(a, b):
    return lax.dot_general(a.astype(BF16), b.astype(BF16), (((0,), (0,)), ((), ())),
                           preferred_element_type=F32)


def _split_dot(x, m01, terms, left):
    acc = None
    rem = x
    for _ in range(terms):
        piece = rem.astype(BF16)
        rem = rem - piece.astype(F32)
        part = (jnp.dot(m01, piece, preferred_element_type=F32) if left
                else jnp.dot(piece, m01, preferred_element_type=F32))
        acc = part if acc is None else acc + part
    return acc


def _sigmoid(x):
    return 1.0 / (1.0 + jnp.exp(-x))


def _rms(x, w):
    return x * lax.rsqrt(jnp.mean(x * x, axis=-1, keepdims=True) + RMS_EPS) * w


def _inproj_body(x_ref, nw_ref, w_ref, p_ref):
    xn = _rms(x_ref[...], nw_ref[...])
    p_ref[...] = jnp.dot(xn.astype(BF16), w_ref[...], preferred_element_type=F32)


def _inproj(x2, nw, w_bf):
    t, d = x2.shape
    c = w_bf.shape[1]
    tm = min(t, 512)
    return pl.pallas_call(
        _inproj_body,
        grid=(t // tm,),
        in_specs=[pl.BlockSpec((tm, d), lambda i: (i, 0)),
                  pl.BlockSpec((1, d), lambda i: (0, 0)),
                  pl.BlockSpec((d, c), lambda i: (0, 0))],
        out_specs=pl.BlockSpec((tm, c), lambda i: (i, 0)),
        out_shape=jax.ShapeDtypeStruct((t, c), F32),
        compiler_params=_params(("parallel",)),
        name="inproj",
    )(x2, nw, w_bf)


def _prep_body(p_ref, convw_ref, mu_ref, w0_ref, wdu_ref, a0_ref, wiu_ref, wgu_ref,
               kk_ref, ka_ref, rk_ref, ones_ref,
               yconv_ref, r_ref, k_ref, v_ref, lw_ref, a_ref, b_ref, g_ref, bonus_ref,
               buf_ref):
    ts = p_ref.shape[0]
    cw, rw = CONV_WIDTH, RWKV_WIDTH
    o_r = 3 * cw

    @pl.when(pl.program_id(1) == 0)
    def _():
        buf_ref[0:8, :] = jnp.zeros((8, buf_ref.shape[1]), F32)

    buf_ref[8:8 + ts, :] = p_ref[...]

    u0 = buf_ref[8:8 + ts, cw:2 * cw] * buf_ref[8:8 + ts, 2 * cw:3 * cw]
    u1 = buf_ref[7:7 + ts, cw:2 * cw] * buf_ref[7:7 + ts, 2 * cw:3 * cw]
    u2 = buf_ref[6:6 + ts, cw:2 * cw] * buf_ref[6:6 + ts, 2 * cw:3 * cw]
    conv = convw_ref[0:1, :] * u2 + convw_ref[1:2, :] * u1 + convw_ref[2:3, :] * u0
    yconv_ref[...] = buf_ref[8:8 + ts, 0:cw] * conv

    def shifted(lo, hi):
        cur = buf_ref[8:8 + ts, o_r + lo:o_r + hi]
        prev = buf_ref[7:7 + ts, o_r + lo:o_r + hi]
        return cur + (prev - cur) * mu_ref[:, lo:hi]

    r = shifted(0, rw)
    k = shifted(rw, 2 * rw)
    v = shifted(2 * rw, 3 * rw)
    o1 = 3 * rw
    d_w = shifted(o1, o1 + LORA_DECAY)
    d_a = shifted(o1 + LORA_DECAY, o1 + LORA_DECAY + LORA_ICLR)
    d_g = shifted(o1 + LORA_DECAY + LORA_ICLR, o1 + LORA_DECAY + LORA_ICLR + LORA_GATE)

    z = -(w0_ref[...] + _mm(jnp.tanh(d_w), wdu_ref[...]))
    softplus = jnp.maximum(z, 0.0) + jnp.log(1.0 + jnp.exp(-jnp.abs(z)))
    w_log = -softplus - 0.5
    lw_ref[...] = -jnp.exp(w_log)
    iclr = _sigmoid(a0_ref[...] + _mm(d_a, wiu_ref[...]))
    g_ref[...] = _mm(_sigmoid(d_g), wgu_ref[...])

    ones = ones_ref[...]
    kk = k * kk_ref[...]
    nrm = jnp.sqrt(_split_dot(kk * kk, ones, 2, left=False))
    kk = kk / jnp.maximum(nrm, 1e-12)
    k2 = k * (1.0 + (iclr - 1.0) * ka_ref[...])
    r_ref[...] = r
    k_ref[...] = k2
    v_ref[...] = v
    a_ref[...] = -kk
    b_ref[...] = kk * iclr
    bonus_ref[...] = _split_dot(r * k2 * rk_ref[...], ones, 2, left=False) * v

    buf_ref[0:8, :] = buf_ref[ts:ts + 8, :]


def _prep(p, bsz, seq, convw, mu, w0, wdu, a0, wiu, wgu, kkw, kaw, rkw, ones):
    t, c = p.shape
    ts = min(seq, 256)
    ns = seq // ts
    row = lambda b, s: (b * ns + s, 0)
    const = lambda b, s: (0, 0)
    full = lambda arr: pl.BlockSpec(arr.shape, const)
    out_spec = pl.BlockSpec((ts, RWKV_WIDTH), row)
    out_sds = jax.ShapeDtypeStruct((t, RWKV_WIDTH), F32)
    return pl.pallas_call(
        _prep_body,
        grid=(bsz, ns),
        in_specs=[pl.BlockSpec((ts, c), row)] + [full(a) for a in
                  (convw, mu, w0, wdu, a0, wiu, wgu, kkw, kaw, rkw, ones)],
        out_specs=[out_spec] * 9,
        out_shape=[out_sds] * 9,
        scratch_shapes=[pltpu.VMEM((ts + 8, c), F32)],
        compiler_params=_params(("parallel", "arbitrary")),
        name="prep",
    )(p, convw, mu, w0, wdu, a0, wiu, wgu, kkw, kaw, rkw, ones)


def _unit_lower_inverse(p_low, eye, bd_mask):
    pd = jnp.where(bd_mask, p_low, 0.0)
    po = p_low - pd
    d = eye + pd
    pk = pd
    for _ in range(3):
        pk = _mm(pk, pk)
        d = d + _mm(d, pk)
    q = _mm(d, po)
    q2 = _mm(q, q)
    w = eye + q + q2 + _mm(q, q2)
    return _mm(w, d)


def _scan_body(r_ref, k_ref, v_ref, lw_ref, a_ref, b_ref, y_ref, s_ref):
    L = CHUNK

    @pl.when(pl.program_id(1) == 0)
    def _():
        s_ref[...] = jnp.zeros(s_ref.shape, F32)

    row = lax.broadcasted_iota(I32, (L, L), 0)
    col = lax.broadcasted_iota(I32, (L, L), 1)
    lower = row >= col
    strict = row > col
    eye = (row == col).astype(F32)
    bd_mask = (row // SUB) == (col // SUB)

    lw = lw_ref[...]
    c = _split_dot(lw, lower.astype(BF16), 3, left=True)
    e_in = jnp.exp(c)
    e_inv = jnp.exp(-c)
    e_ex = jnp.exp(c - lw)
    g_last = e_in[L - 1:L, :]
    a_t = a_ref[...] * e_ex
    r_t = r_ref[...] * e_in
    b_t = b_ref[...] * e_inv
    k_t = k_ref[...] * e_inv
    b_h = b_t * g_last
    k_h = k_t * g_last
    v = v_ref[...]

    for h in range(HEADS):
        sl = slice(h * HEAD, (h + 1) * HEAD)
        ar = jnp.concatenate([a_t[:, sl], r_t[:, sl]], axis=0)
        gb = _mm_nt(ar, b_t[:, sl])
        gk = _mm_nt(ar, k_t[:, sl])
        p_low = jnp.where(strict, gb[0:L], 0.0)
        m_ak = jnp.where(strict, gk[0:L], 0.0)
        m_br = jnp.where(lower, gb[L:2 * L], 0.0)
        m_kr = jnp.where(lower, gk[L:2 * L], 0.0)
        t_inv = _unit_lower_inverse(p_low, eye, bd_mask)

        s_old = s_ref[h]
        vh = v[:, sl]
        ars = _mm_nt(ar, s_old)
        mv = _mm(jnp.concatenate([m_ak, m_kr], axis=0), vh)
        u = _mm(t_inv, ars[0:L] + mv[0:L])
        y = ars[L:2 * L] + _mm(m_br, u) + mv[L:2 * L]
        uv = jnp.concatenate([u, vh], axis=0)
        bk = jnp.concatenate([b_h[:, sl], k_h[:, sl]], axis=0)
        s_ref[h] = s_old * g_last[:, sl] + _mm_tn(uv, bk)

        mean = jnp.mean(y, axis=-1, keepdims=True)
        yc = y - mean
        var = jnp.mean(yc * yc, axis=-1, keepdims=True)
        y_ref[:, sl] = yc * lax.rsqrt(var + GN_EPS)


def _scan(r, k, v, lw, a, b, bsz, seq):
    t = r.shape[0]
    nc = seq // CHUNK
    spec = pl.BlockSpec((CHUNK, RWKV_WIDTH), lambda bi, ci: (bi * nc + ci, 0))
    return pl.pallas_call(
        _scan_body,
        grid=(bsz, nc),
        in_specs=[spec] * 6,
        out_specs=spec,
        out_shape=jax.ShapeDtypeStruct((t, RWKV_WIDTH), F32),
        scratch_shapes=[pltpu.VMEM((HEADS, HEAD, HEAD), F32)],
        compiler_params=_params(("parallel", "arbitrary")),
        name="scan",
    )(r, k, v, lw, a, b)


def _mixout_body(yn_ref, bonus_ref, g_ref, yconv_ref, x_ref, lnw_ref, lnb_ref, wout_ref,
                 nfw_ref, rwt_ref, rb_ref,
                 h_ref, xn_ref, idx_ref, gate_ref, rank_ref, cnt_ref, carry_ref):
    tm = x_ref.shape[0]

    @pl.when(pl.program_id(0) == 0)
    def _():
        carry_ref[...] = jnp.zeros(carry_ref.shape, F32)

    y_rwkv = (yn_ref[...] * lnw_ref[...] + lnb_ref[...] + bonus_ref[...]) * g_ref[...]
    cat = jnp.concatenate([yconv_ref[...], y_rwkv], axis=1)
    h = x_ref[...] + jnp.dot(cat.astype(BF16), wout_ref[...], preferred_element_type=F32)
    h_ref[...] = h
    xn = _rms(h, nfw_ref[...])
    xn_ref[...] = xn

    lt = lax.dot_general(rwt_ref[...], xn, (((1,), (1,)), ((), ())),
                         precision=lax.Precision.HIGHEST, preferred_element_type=F32)
    lt = lt + rb_ref[...]
    eio = lax.broadcasted_iota(I32, lt.shape, 0)
    vals, onehots = [], []
    cur = lt
    for kk in range(TOP_K):
        m = jnp.max(cur, axis=0, keepdims=True)
        ix = jnp.min(jnp.where(cur == m, eio, N_EXPERTS), axis=0, keepdims=True)
        sel = eio == ix
        idx_ref[kk:kk + 1, :] = ix
        vals.append(m)
        onehots.append(sel.astype(F32))
        cur = jnp.where(sel, -jnp.inf, cur)
    exps = [jnp.exp(vv - vals[0]) for vv in vals]
    den = exps[0] + exps[1] + exps[2] + exps[3]
    for kk in range(TOP_K):
        gate_ref[kk:kk + 1, :] = exps[kk] / den

    trow = lax.broadcasted_iota(I32, (tm, tm), 0)
    tcol = lax.broadcasted_iota(I32, (tm, tm), 1)
    tri = (trow <= tcol).astype(BF16)
    ohs = jnp.concatenate(onehots, axis=0).astype(BF16)
    pref = jnp.dot(ohs, tri, preferred_element_type=F32)
    base = carry_ref[...]
    for kk in range(TOP_K):
        pk = pref[kk * N_EXPERTS:(kk + 1) * N_EXPERTS]
        rk = jnp.sum(onehots[kk] * (pk - 1.0 + base), axis=0, keepdims=True)
        rank_ref[kk:kk + 1, :] = rk.astype(I32)
        base = base + jnp.sum(onehots[kk], axis=1, keepdims=True)
    carry_ref[...] = base
    cnt_ref[...] = base


def _mixout(yn, bonus, g, yconv, x2, lnw, lnb, wout_bf, nfw, rwt, rb):
    t, d = x2.shape
    tm = min(t, 512)
    row = lambda i: (i, 0)
    const = lambda i: (0, 0)
    colb = lambda i: (0, i)
    half = pl.BlockSpec((tm, RWKV_WIDTH), row)
    full = lambda arr: pl.BlockSpec(arr.shape, const)
    return pl.pallas_call(
        _mixout_body,
        grid=(t // tm,),
        in_specs=[half, half, half, half, pl.BlockSpec((tm, d), row),
                  full(lnw), full(lnb), full(wout_bf), full(nfw), full(rwt), full(rb)],
        out_specs=[pl.BlockSpec((tm, d), row), pl.BlockSpec((tm, d), row),
                   pl.BlockSpec((TOP_K, tm), colb), pl.BlockSpec((TOP_K, tm), colb),
                   pl.BlockSpec((TOP_K, tm), colb), pl.BlockSpec((N_EXPERTS, 1), const)],
        out_shape=[jax.ShapeDtypeStruct((t, d), F32), jax.ShapeDtypeStruct((t, d), F32),
                   jax.ShapeDtypeStruct((TOP_K, t), I32), jax.ShapeDtypeStruct((TOP_K, t), F32),
                   jax.ShapeDtypeStruct((TOP_K, t), I32),
                   jax.ShapeDtypeStruct((N_EXPERTS, 1), F32)],
        scratch_shapes=[pltpu.VMEM((N_EXPERTS, 1), F32)],
        compiler_params=_params(("arbitrary",)),
        name="mix_out",
    )(yn, bonus, g, yconv, x2, lnw, lnb, wout_bf, nfw, rwt, rb)


def _gather_step(i, n_live, idx_hbm, src_hbm, idx_smem, rows, isem, gsem):
    n = rows.shape[1]

    def idx_copy(blk):
        return pltpu.make_async_copy(idx_hbm.at[blk], idx_smem.at[blk % 2], isem.at[blk % 2])

    def row_copy(slot, src_row, r):
        return pltpu.make_async_copy(src_hbm.at[pl.ds(src_row, 1)], rows.at[slot, pl.ds(r, 1)],
                                     gsem.at[slot])

    def issue_rows(blk):
        slot = blk % 2

        def body(r, carry):
            row_copy(slot, idx_smem[slot, r], r).start()
            return carry

        lax.fori_loop(0, n, body, 0, unroll=8)

    @pl.when(i == 0)
    def _():
        first = idx_copy(0)
        first.start()
        first.wait()
        issue_rows(0)

        @pl.when(n_live > 1)
        def _():
            idx_copy(1).start()

    @pl.when(i + 1 < n_live)
    def _():
        idx_copy(i + 1).wait()
        issue_rows(i + 1)

        @pl.when(i + 2 < n_live)
        def _():
            idx_copy(i + 2).start()

    @pl.when(i < n_live)
    def _():
        slot = i % 2

        def body(r, carry):
            row_copy(slot, 0, r).wait()
            return carry

        lax.fori_loop(0, n, body, 0, unroll=8)


def _expert_body(blk_e_ref, nused_ref, tok_hbm, xn_hbm, wg_ref, wl_ref, bg_ref, bl_ref,
                 wd_ref, bd_ref, y_ref, xbuf, tok_smem, isem, gsem):
    i = pl.program_id(0)
    n_live = nused_ref[0]
    _gather_step(i, n_live, tok_hbm, xn_hbm, tok_smem, xbuf, isem, gsem)

    @pl.when(i < n_live)
    def _():
        xb = xbuf[i % 2].astype(BF16)
        hg = jnp.dot(xb, wg_ref[...], preferred_element_type=F32) + bg_ref[...]
        hl = jnp.dot(xb, wl_ref[...], preferred_element_type=F32) + bl_ref[...]
        x_glu = jnp.minimum(hg, SWIGLU_LIMIT)
        x_lin = jnp.clip(hl, -SWIGLU_LIMIT, SWIGLU_LIMIT)
        act = x_glu * _sigmoid(SWIGLU_ALPHA * x_glu) * (x_lin + 1.0)
        y_ref[...] = jnp.dot(act.astype(BF16), wd_ref[...], preferred_element_type=F32) + bd_ref[...]

    @pl.when(i >= n_live)
    def _():
        y_ref[...] = jnp.zeros(y_ref.shape, F32)


def _experts(blk_e, nused, row_tok, xn2, wg, wl, bg, bl, wd, bd):
    nb, bm = row_tok.shape
    d = xn2.shape[1]
    f = wg.shape[2]
    wspec = lambda shape: pl.BlockSpec((None,) + shape, lambda i, be, nu: (be[i], 0, 0))
    grid_spec = pltpu.PrefetchScalarGridSpec(
        num_scalar_prefetch=2,
        grid=(nb,),
        in_specs=[pl.BlockSpec(memory_space=pl.ANY), pl.BlockSpec(memory_space=pl.ANY),
                  wspec((d, f)), wspec((d, f)), wspec((1, f)), wspec((1, f)),
                  wspec((f, d)), wspec((1, d))],
        out_specs=pl.BlockSpec((bm, d), lambda i, be, nu: (i, 0)),
        scratch_shapes=[pltpu.VMEM((2, bm, d), F32), pltpu.SMEM((2, bm), I32),
                        pltpu.SemaphoreType.DMA((2,)), pltpu.SemaphoreType.DMA((2,))],
    )
    return pl.pallas_call(
        _expert_body,
        grid_spec=grid_spec,
        out_shape=jax.ShapeDtypeStruct((nb * bm, d), F32),
        compiler_params=_params(("arbitrary",)),
        name="experts",
    )(blk_e, nused, row_tok, xn2, wg, wl, bg, bl, wd, bd)


def _combine_body(dest_hbm, yb_hbm, h_ref, gate_ref, nw_ref, o_ref, ybuf, dest_smem, isem, gsem):
    i = pl.program_id(0)
    tm = h_ref.shape[0]
    _gather_step(i, pl.num_programs(0), dest_hbm, yb_hbm, dest_smem, ybuf, isem, gsem)
    slot = i % 2
    acc = h_ref[...]
    gates = gate_ref[...]
    for kk in range(TOP_K):
        acc = acc + gates[:, kk:kk + 1] * ybuf[slot, kk * tm:(kk + 1) * tm, :]
    o_ref[...] = _rms(acc, nw_ref[...])


def _combine(dest_tiles, yb, h, gates_t, nfw):
    t, d = h.shape
    nt, n = dest_tiles.shape
    tm = n // TOP_K
    row = lambda i: (i, 0)
    return pl.pallas_call(
        _combine_body,
        grid=(nt,),
        in_specs=[pl.BlockSpec(memory_space=pl.ANY), pl.BlockSpec(memory_space=pl.ANY),
                  pl.BlockSpec((tm, d), row), pl.BlockSpec((tm, TOP_K), row),
                  pl.BlockSpec((1, d), lambda i: (0, 0))],
        out_specs=pl.BlockSpec((tm, d), row),
        out_shape=jax.ShapeDtypeStruct((t, d), F32),
        scratch_shapes=[pltpu.VMEM((2, n, d), F32), pltpu.SMEM((2, n), I32),
                        pltpu.SemaphoreType.DMA((2,)), pltpu.SemaphoreType.DMA((2,))],
        compiler_params=_params(("arbitrary",)),
        name="combine",
    )(dest_tiles, yb, h, gates_t, nfw)


def _head_ones():
    hi = lax.broadcasted_iota(I32, (RWKV_WIDTH, RWKV_WIDTH), 0) // HEAD
    hj = lax.broadcasted_iota(I32, (RWKV_WIDTH, RWKV_WIDTH), 1) // HEAD
    return (hi == hj).astype(BF16)


def _layer(x2, bsz, seq, norm_mix_w, w_in, conv_w, mu_shift, w0, w_decay_up, a0, w_iclr_up,
           w_gate_up, k_k, k_a, r_k, ln_x_w, ln_x_b, w_out, norm_ffn_w, router_w, router_b,
           gu_w, gu_b, dn_w, dn_b):
    t, d = x2.shape
    row1 = lambda a: a.reshape(1, -1)

    p = _inproj(x2, row1(norm_mix_w), w_in.astype(BF16))
    yconv, r, k, v, lw, a, b, g, bonus = _prep(
        p, bsz, seq, conv_w, row1(mu_shift), row1(w0), w_decay_up, row1(a0), w_iclr_up,
        w_gate_up, row1(k_k), row1(k_a), row1(r_k), _head_ones())
    yn = _scan(r, k, v, lw, a, b, bsz, seq)
    h, xn2, idx, gates, rank, cnt = _mixout(
        yn, bonus, g, yconv, x2, row1(ln_x_w), row1(ln_x_b), w_out.astype(BF16),
        row1(norm_ffn_w), router_w.T, router_b.reshape(-1, 1))

    bm = ROW_BLOCK
    counts = cnt[:, 0].astype(I32)
    pcounts = (counts + bm - 1) // bm * bm
    pends = jnp.cumsum(pcounts)
    pstarts = pends - pcounts
    dest = pstarts[idx] + rank
    n_rows = t * TOP_K + N_EXPERTS * bm
    n_blocks = n_rows // bm
    tok_of_pair = jnp.tile(jnp.arange(t, dtype=I32), TOP_K)
    row_tok = jnp.zeros((n_rows,), I32).at[dest.reshape(-1)].set(tok_of_pair, unique_indices=True)
    blk_e = jnp.minimum(
        jnp.searchsorted(pends, jnp.arange(n_blocks, dtype=I32) * bm, side='right'),
        N_EXPERTS - 1).astype(I32)
    nused = (pends[-1] // bm).astype(I32).reshape(1)

    f = gu_w.shape[2] // 2
    wg = gu_w[:, :, 0::2].astype(BF16)
    wl = gu_w[:, :, 1::2].astype(BF16)
    bg = gu_b[:, 0::2].reshape(N_EXPERTS, 1, f)
    bl = gu_b[:, 1::2].reshape(N_EXPERTS, 1, f)
    yb = _experts(blk_e, nused, row_tok.reshape(n_blocks, bm), xn2, wg, wl, bg, bl,
                  dn_w.astype(BF16), dn_b.reshape(N_EXPERTS, 1, d))

    tm = min(t, 256)
    nt = t // tm
    dest_tiles = dest.reshape(TOP_K, nt, tm).transpose(1, 0, 2).reshape(nt, TOP_K * tm)
    return dest_tiles, yb, h, gates.T


def kernel(x, norm_mix_w, w_in, conv_w, mu_shift, w0, w_decay_up, a0, w_iclr_up, w_gate_up,
           k_k, k_a, r_k, ln_x_w, ln_x_b, w_out, norm_ffn_w, router_w, router_b,
           gu_w, gu_b, dn_w, dn_b, norm_final_w):
    bsz, seq, d = x.shape
    assert w_in.shape[0] == 1, "the final RMSNorm is fused into the single layer's combine step"
    dest_tiles, yb, h_mid, gates_t = _layer(
        x.reshape(bsz * seq, d), bsz, seq, norm_mix_w[0], w_in[0], conv_w[0], mu_shift[0], w0[0],
        w_decay_up[0], a0[0], w_iclr_up[0], w_gate_up[0], k_k[0], k_a[0], r_k[0], ln_x_w[0],
        ln_x_b[0], w_out[0], norm_ffn_w[0], router_w[0], router_b[0], gu_w[0], gu_b[0], dn_w[0],
        dn_b[0])
    out = _combine(dest_tiles, yb, h_mid, gates_t, norm_final_w.reshape(1, d))
    return out.reshape(bsz, seq, d)
```

```python
import functools

import jax
import jax.numpy as jnp
from jax import lax
from jax.experimental import pallas as pl
from jax.experimental.pallas import tpu as pltpu

F32 = jnp.float32
BF16 = jnp.bfloat16
I32 = jnp.int32

CONV_WIDTH = 512
CONV_K = 3
RWKV_WIDTH = 512
HEAD = 64
HEADS = RWKV_WIDTH // HEAD
LORA_DECAY = 64
LORA_ICLR = 64
LORA_GATE = 128
N_EXPERTS = 32
TOP_K = 4
SWIGLU_LIMIT = 7.0
SWIGLU_ALPHA = 1.702
RMS_EPS = 1e-5
GN_EPS = 64e-5

CHUNK = 64
SUB = 16
ROW_BLOCK = 256
LANES = 128
VMEM_LIMIT = 56 * 1024 * 1024


def _params(sem, vmem=VMEM_LIMIT):
    return pltpu.CompilerParams(dimension_semantics=sem, vmem_limit_bytes=vmem)


def _mm(a, b):
    return jnp.dot(a.astype(BF16), b.astype(BF16), preferred_element_type=F32)


def _mm_nt(a, b):
    return lax.dot_general(a.astype(BF16), b.astype(BF16), (((1,), (1,)), ((), ())),
                           preferred_element_type=F32)


def _mm_tn(a, b):
    return lax.dot_general(a.astype(BF16), b.astype(BF16), (((0,), (0,)), ((), ())),
                           preferred_element_type=F32)


def _split_dot(x, m01, terms, left):
    acc = None
    rem = x
    for _ in range(terms):
        piece = rem.astype(BF16)
        rem = rem - piece.astype(F32)
        part = (jnp.dot(m01, piece, preferred_element_type=F32) if left
                else jnp.dot(piece, m01, preferred_element_type=F32))
        acc = part if acc is None else acc + part
    return acc


def _sigmoid(x):
    return 1.0 / (1.0 + jnp.exp(-x))


def _rms(x, w):
    return x * lax.rsqrt(jnp.mean(x * x, axis=-1, keepdims=True) + RMS_EPS) * w


def _inproj_body(x_ref, nw_ref, w_ref, p_ref):
    xn = _rms(x_ref[...], nw_ref[...])
    p_ref[...] = jnp.dot(xn.astype(BF16), w_ref[...], preferred_element_type=F32)


def _inproj(x2, nw, w_bf):
    t, d = x2.shape
    c = w_bf.shape[1]
    tm = min(t, 512)
    return pl.pallas_call(
        _inproj_body,
        grid=(t // tm,),
        in_specs=[pl.BlockSpec((tm, d), lambda i: (i, 0)),
                  pl.BlockSpec((1, d), lambda i: (0, 0)),
                  pl.BlockSpec((d, c), lambda i: (0, 0))],
        out_specs=pl.BlockSpec((tm, c), lambda i: (i, 0)),
        out_shape=jax.ShapeDtypeStruct((t, c), F32),
        compiler_params=_params(("parallel",)),
        name="inproj",
    )(x2, nw, w_bf)


def _prep_body(p_ref, convw_ref, mu_ref, w0_ref, wdu_ref, a0_ref, wiu_ref, wgu_ref,
               kk_ref, ka_ref, rk_ref, ones_ref,
               yconv_ref, r_ref, k_ref, v_ref, lw_ref, a_ref, b_ref, g_ref, bonus_ref,
               buf_ref):
    ts = p_ref.shape[0]
    cw, rw = CONV_WIDTH, RWKV_WIDTH
    o_r = 3 * cw

    @pl.when(pl.program_id(1) == 0)
    def _():
        buf_ref[0:8, :] = jnp.zeros((8, buf_ref.shape[1]), F32)

    buf_ref[8:8 + ts, :] = p_ref[...]

    u0 = buf_ref[8:8 + ts, cw:2 * cw] * buf_ref[8:8 + ts, 2 * cw:3 * cw]
    u1 = buf_ref[7:7 + ts, cw:2 * cw] * buf_ref[7:7 + ts, 2 * cw:3 * cw]
    u2 = buf_ref[6:6 + ts, cw:2 * cw] * buf_ref[6:6 + ts, 2 * cw:3 * cw]
    conv = convw_ref[0:1, :] * u2 + convw_ref[1:2, :] * u1 + convw_ref[2:3, :] * u0
    yconv_ref[...] = buf_ref[8:8 + ts, 0:cw] * conv

    def shifted(lo, hi):
        cur = buf_ref[8:8 + ts, o_r + lo:o_r + hi]
        prev = buf_ref[7:7 + ts, o_r + lo:o_r + hi]
        return cur + (prev - cur) * mu_ref[:, lo:hi]

    r = shifted(0, rw)
    k = shifted(rw, 2 * rw)
    v = shifted(2 * rw, 3 * rw)
    o1 = 3 * rw
    d_w = shifted(o1, o1 + LORA_DECAY)
    d_a = shifted(o1 + LORA_DECAY, o1 + LORA_DECAY + LORA_ICLR)
    d_g = shifted(o1 + LORA_DECAY + LORA_ICLR, o1 + LORA_DECAY + LORA_ICLR + LORA_GATE)

    z = -(w0_ref[...] + _mm(jnp.tanh(d_w), wdu_ref[...]))
    softplus = jnp.maximum(z, 0.0) + jnp.log(1.0 + jnp.exp(-jnp.abs(z)))
    w_log = -softplus - 0.5
    lw_ref[...] = -jnp.exp(w_log)
    iclr = _sigmoid(a0_ref[...] + _mm(d_a, wiu_ref[...]))
    g_ref[...] = _mm(_sigmoid(d_g), wgu_ref[...])

    ones = ones_ref[...]
    kk = k * kk_ref[...]
    nrm = jnp.sqrt(_split_dot(kk * kk, ones, 2, left=False))
    kk = kk / jnp.maximum(nrm, 1e-12)
    k2 = k * (1.0 + (iclr - 1.0) * ka_ref[...])
    r_ref[...] = r
    k_ref[...] = k2
    v_ref[...] = v
    a_ref[...] = -kk
    b_ref[...] = kk * iclr
    bonus_ref[...] = _split_dot(r * k2 * rk_ref[...], ones, 2, left=False) * v

    buf_ref[0:8, :] = buf_ref[ts:ts + 8, :]


def _prep(p, bsz, seq, convw, mu, w0, wdu, a0, wiu, wgu, kkw, kaw, rkw, ones):
    t, c = p.shape
    ts = min(seq, 256)
    ns = seq // ts
    row = lambda b, s: (b * ns + s, 0)
    const = lambda b, s: (0, 0)
    full = lambda arr: pl.BlockSpec(arr.shape, const)
    out_spec = pl.BlockSpec((ts, RWKV_WIDTH), row)
    out_sds = jax.ShapeDtypeStruct((t, RWKV_WIDTH), F32)
    return pl.pallas_call(
        _prep_body,
        grid=(bsz, ns),
        in_specs=[pl.BlockSpec((ts, c), row)] + [full(a) for a in
                  (convw, mu, w0, wdu, a0, wiu, wgu, kkw, kaw, rkw, ones)],
        out_specs=[out_spec] * 9,
        out_shape=[out_sds] * 9,
        scratch_shapes=[pltpu.VMEM((ts + 8, c), F32)],
        compiler_params=_params(("parallel", "arbitrary")),
        name="prep",
    )(p, convw, mu, w0, wdu, a0, wiu, wgu, kkw, kaw, rkw, ones)


def _unit_lower_inverse(p_low, eye, bd_mask):
    pd = [jnp.where(bd_mask, p, 0.0) for p in p_low]
    po = [p - x for p, x in zip(p_low, pd)]
    d = [eye + x for x in pd]
    pk = pd
    for _ in range(3):
        pk = [_mm(x, x) for x in pk]
        dd = [_mm(x, y) for x, y in zip(d, pk)]
        d = [x + y for x, y in zip(d, dd)]
    q = [_mm(x, y) for x, y in zip(d, po)]
    q2 = [_mm(x, x) for x in q]
    q3 = [_mm(x, y) for x, y in zip(q, q2)]
    w = [eye + a + b + c for a, b, c in zip(q, q2, q3)]
    return [_mm(x, y) for x, y in zip(w, d)]


def _scan_body(r_ref, k_ref, v_ref, lw_ref, a_ref, b_ref, y_ref, s_ref):
    L = CHUNK

    @pl.when(pl.program_id(1) == 0)
    def _():
        s_ref[...] = jnp.zeros(s_ref.shape, F32)

    row = lax.broadcasted_iota(I32, (L, L), 0)
    col = lax.broadcasted_iota(I32, (L, L), 1)
    lower = row >= col
    strict = row > col
    eye = (row == col).astype(F32)
    bd_mask = (row // SUB) == (col // SUB)

    lw = lw_ref[...]
    c = _split_dot(lw, lower.astype(BF16), 3, left=True)
    e_in = jnp.exp(c)
    e_inv = jnp.exp(-c)
    e_ex = jnp.exp(c - lw)
    g_last = e_in[L - 1:L, :]
    a_t = a_ref[...] * e_ex
    r_t = r_ref[...] * e_in
    b_t = b_ref[...] * e_inv
    k_t = k_ref[...] * e_inv
    b_h = b_t * g_last
    k_h = k_t * g_last
    v = v_ref[...]

    hs = range(HEADS)
    sls = [slice(h * HEAD, (h + 1) * HEAD) for h in hs]
    ar = [jnp.concatenate([a_t[:, s], r_t[:, s]], axis=0) for s in sls]
    s_old = [s_ref[h] for h in hs]
    ars = [_mm_nt(ar[h], s_old[h]) for h in hs]
    gb = [_mm_nt(ar[h], b_t[:, sls[h]]) for h in hs]
    gk = [_mm_nt(ar[h], k_t[:, sls[h]]) for h in hs]
    p_low = [jnp.where(strict, x[0:L], 0.0) for x in gb]
    m_br = [jnp.where(lower, x[L:2 * L], 0.0) for x in gb]
    m_akr = [jnp.concatenate([jnp.where(strict, x[0:L], 0.0), jnp.where(lower, x[L:2 * L], 0.0)],
                             axis=0) for x in gk]
    vh = [v[:, s] for s in sls]
    mv = [_mm(m_akr[h], vh[h]) for h in hs]
    t_inv = _unit_lower_inverse(p_low, eye, bd_mask)
    u = [_mm(t_inv[h], ars[h][0:L] + mv[h][0:L]) for h in hs]
    mu = [_mm(m_br[h], u[h]) for h in hs]
    s_add = [_mm_tn(jnp.concatenate([u[h], vh[h]], axis=0),
                    jnp.concatenate([b_h[:, sls[h]], k_h[:, sls[h]]], axis=0)) for h in hs]
    for h in hs:
        s_ref[h] = s_old[h] * g_last[:, sls[h]] + s_add[h]
        y = ars[h][L:2 * L] + mu[h] + mv[h][L:2 * L]
        mean = jnp.mean(y, axis=-1, keepdims=True)
        yc = y - mean
        var = jnp.mean(yc * yc, axis=-1, keepdims=True)
        y_ref[:, sls[h]] = yc * lax.rsqrt(var + GN_EPS)


def _scan(r, k, v, lw, a, b, bsz, seq):
    t = r.shape[0]
    nc = seq // CHUNK
    spec = pl.BlockSpec((CHUNK, RWKV_WIDTH), lambda bi, ci: (bi * nc + ci, 0))
    return pl.pallas_call(
        _scan_body,
        grid=(bsz, nc),
        in_specs=[spec] * 6,
        out_specs=spec,
        out_shape=jax.ShapeDtypeStruct((t, RWKV_WIDTH), F32),
        scratch_shapes=[pltpu.VMEM((HEADS, HEAD, HEAD), F32)],
        compiler_params=_params(("parallel", "arbitrary")),
        name="scan",
    )(r, k, v, lw, a, b)


def _mixout_body(yn_ref, bonus_ref, g_ref, yconv_ref, x_ref, lnw_ref, lnb_ref, wout_ref,
                 nfw_ref, rwt_ref, rb_ref,
                 h_ref, xn_ref, idx_ref, gate_ref, rank_ref, cnt_ref, carry_ref):
    tm = x_ref.shape[0]

    @pl.when(pl.program_id(0) == 0)
    def _():
        carry_ref[...] = jnp.zeros(carry_ref.shape, F32)

    y_rwkv = (yn_ref[...] * lnw_ref[...] + lnb_ref[...] + bonus_ref[...]) * g_ref[...]
    cat = jnp.concatenate([yconv_ref[...], y_rwkv], axis=1)
    h = x_ref[...] + jnp.dot(cat.astype(BF16), wout_ref[...], preferred_element_type=F32)
    h_ref[...] = h
    xn = _rms(h, nfw_ref[...])
    xn_ref[...] = xn

    lt = lax.dot_general(rwt_ref[...], xn, (((1,), (1,)), ((), ())),
                         precision=lax.Precision.HIGHEST, preferred_element_type=F32)
    lt = lt + rb_ref[...]
    eio = lax.broadcasted_iota(I32, lt.shape, 0)
    vals, onehots = [], []
    cur = lt
    for kk in range(TOP_K):
        m = jnp.max(cur, axis=0, keepdims=True)
        ix = jnp.min(jnp.where(cur == m, eio, N_EXPERTS), axis=0, keepdims=True)
        sel = eio == ix
        idx_ref[kk:kk + 1, :] = ix
        vals.append(m)
        onehots.append(sel.astype(F32))
        cur = jnp.where(sel, -jnp.inf, cur)
    exps = [jnp.exp(vv - vals[0]) for vv in vals]
    den = exps[0] + exps[1] + exps[2] + exps[3]
    for kk in range(TOP_K):
        gate_ref[kk:kk + 1, :] = exps[kk] / den

    trow = lax.broadcasted_iota(I32, (tm, tm), 0)
    tcol = lax.broadcasted_iota(I32, (tm, tm), 1)
    tri = (trow <= tcol).astype(BF16)
    ohs = jnp.concatenate(onehots, axis=0).astype(BF16)
    pref = jnp.dot(ohs, tri, preferred_element_type=F32)
    base = carry_ref[...]
    for kk in range(TOP_K):
        pk = pref[kk * N_EXPERTS:(kk + 1) * N_EXPERTS]
        rk = jnp.sum(onehots[kk] * (pk - 1.0 + base), axis=0, keepdims=True)
        rank_ref[kk:kk + 1, :] = rk.astype(I32)
        base = base + jnp.sum(onehots[kk], axis=1, keepdims=True)
    carry_ref[...] = base
    cnt_ref[...] = base


def _mixout(yn, bonus, g, yconv, x2, lnw, lnb, wout_bf, nfw, rwt, rb):
    t, d = x2.shape
    tm = min(t, 512)
    row = lambda i: (i, 0)
    const = lambda i: (0, 0)
    colb = lambda i: (0, i)
    half = pl.BlockSpec((tm, RWKV_WIDTH), row)
    full = lambda arr: pl.BlockSpec(arr.shape, const)
    return pl.pallas_call(
        _mixout_body,
        grid=(t // tm,),
        in_specs=[half, half, half, half, pl.BlockSpec((tm, d), row),
                  full(lnw), full(lnb), full(wout_bf), full(nfw), full(rwt), full(rb)],
        out_specs=[pl.BlockSpec((tm, d), row), pl.BlockSpec((tm, d), row),
                   pl.BlockSpec((TOP_K, tm), colb), pl.BlockSpec((TOP_K, tm), colb),
                   pl.BlockSpec((TOP_K, tm), colb), pl.BlockSpec((N_EXPERTS, 1), const)],
        out_shape=[jax.ShapeDtypeStruct((t, d), F32), jax.ShapeDtypeStruct((t, d), F32),
                   jax.ShapeDtypeStruct((TOP_K, t), I32), jax.ShapeDtypeStruct((TOP_K, t), F32),
                   jax.ShapeDtypeStruct((TOP_K, t), I32),
                   jax.ShapeDtypeStruct((N_EXPERTS, 1), F32)],
        scratch_shapes=[pltpu.VMEM((N_EXPERTS, 1), F32)],
        compiler_params=_params(("arbitrary",)),
        name="mix_out",
    )(yn, bonus, g, yconv, x2, lnw, lnb, wout_bf, nfw, rwt, rb)


class _RowRing:
    def __init__(self, idx_hbm, src_hbm, idx_smem, rows, isem, gsem):
        self.idx_hbm, self.src_hbm, self.idx_smem, self.rows = idx_hbm, src_hbm, idx_smem, rows
        self.isem, self.gsem = isem, gsem
        self.n = rows.shape[1]

    def idx_copy(self, blk):
        return pltpu.make_async_copy(self.idx_hbm.at[blk], self.idx_smem.at[blk % 2],
                                     self.isem.at[blk % 2])

    def row_copy(self, slot, src_row, r):
        return pltpu.make_async_copy(self.src_hbm.at[pl.ds(src_row, 1)],
                                     self.rows.at[slot, pl.ds(r, 1)], self.gsem.at[slot])

    def issue_rows(self, blk, unrolled):
        slot = blk % 2
        if unrolled:
            for r in range(self.n):
                self.row_copy(slot, self.idx_smem[slot, r], r).start()
        else:
            def body(r, carry):
                self.row_copy(slot, self.idx_smem[slot, r], r).start()
                return carry
            lax.fori_loop(0, self.n, body, 0, unroll=8)

    def wait_rows(self, blk):
        slot = blk % 2
        for r in range(self.n):
            self.row_copy(slot, 0, r).wait()

    def prologue(self):
        first = self.idx_copy(0)
        first.start()
        first.wait()
        self.issue_rows(0, unrolled=False)
        self.idx_copy(1).start()


def _expert_body(blk_e_ref, nused_ref, tok_hbm, xn_hbm, gu_ref, gub_ref, dn_ref, dnb_ref, perm_ref,
                 y_ref, xbuf, wgl, wd, tok_smem, isem, gsem):
    i = pl.program_id(0)
    n_live = nused_ref[0]
    ring = _RowRing(tok_hbm, xn_hbm, tok_smem, xbuf, isem, gsem)
    grp = perm_ref.shape[0]
    half = grp // 2

    @pl.when(i == 0)
    def _():
        ring.prologue()

    @pl.when(i + 2 <= n_live)
    def _():
        ring.idx_copy(i + 2).start()

    new_expert = (i == 0) | (blk_e_ref[i] != blk_e_ref[jnp.maximum(i - 1, 0)])

    @pl.when((i < n_live) & new_expert)
    def _():
        for c in range(gu_ref.shape[1] // grp):
            cols = slice(c * grp, (c + 1) * grp)
            wgl[:, cols] = jnp.dot(gu_ref[:, cols].astype(BF16), perm_ref[...],
                                   preferred_element_type=F32).astype(BF16)
        wd[...] = dn_ref[...].astype(BF16)

    @pl.when(i < n_live)
    def _():
        ring.idx_copy(i + 1).wait()
        ring.issue_rows(i + 1, unrolled=True)
        ring.wait_rows(i)
        xb = xbuf[i % 2].astype(BF16)
        h = jnp.dot(xb, wgl[...], preferred_element_type=F32) + gub_ref[...]
        acts = []
        for c in range(h.shape[1] // grp):
            x_glu = jnp.minimum(h[:, c * grp:c * grp + half], SWIGLU_LIMIT)
            x_lin = jnp.clip(h[:, c * grp + half:(c + 1) * grp], -SWIGLU_LIMIT, SWIGLU_LIMIT)
            acts.append(x_glu * _sigmoid(SWIGLU_ALPHA * x_glu) * (x_lin + 1.0))
        act = jnp.concatenate(acts, axis=1)
        y_ref[...] = jnp.dot(act.astype(BF16), wd[...], preferred_element_type=F32) + dnb_ref[...]

    @pl.when(i == n_live)
    def _():
        ring.wait_rows(i)

    @pl.when(i >= n_live)
    def _():
        y_ref[...] = jnp.zeros(y_ref.shape, F32)


def _experts(blk_e, nused, row_tok, xn2, gu_w, gu_b, dn_w, dn_b, perm):
    nb, bm = row_tok.shape
    d = xn2.shape[1]
    f2 = gu_w.shape[2]
    f = dn_w.shape[1]
    wspec = lambda shape: pl.BlockSpec((None,) + shape, lambda i, be, nu: (be[i], 0, 0))
    grid_spec = pltpu.PrefetchScalarGridSpec(
        num_scalar_prefetch=2,
        grid=(nb,),
        in_specs=[pl.BlockSpec(memory_space=pl.ANY), pl.BlockSpec(memory_space=pl.ANY),
                  wspec((d, f2)), wspec((1, f2)), wspec((f, d)), wspec((1, d)),
                  pl.BlockSpec(perm.shape, lambda i, be, nu: (0, 0))],
        out_specs=pl.BlockSpec((bm, d), lambda i, be, nu: (i, 0)),
        scratch_shapes=[pltpu.VMEM((2, bm, d), F32), pltpu.VMEM((d, f2), BF16),
                        pltpu.VMEM((f, d), BF16), pltpu.SMEM((2, bm), I32),
                        pltpu.SemaphoreType.DMA((2,)), pltpu.SemaphoreType.DMA((2,))],
    )
    return pl.pallas_call(
        _expert_body,
        grid_spec=grid_spec,
        out_shape=jax.ShapeDtypeStruct((nb * bm, d), F32),
        compiler_params=_params(("arbitrary",)),
        name="experts",
    )(blk_e, nused, row_tok, xn2, gu_w, gu_b, dn_w, dn_b, perm)


def _combine_body(dest_hbm, yb_hbm, h_ref, gate_ref, nw_ref, o_ref, ybuf, dest_smem, isem, gsem):
    i = pl.program_id(0)
    nt = pl.num_programs(0)
    tm = h_ref.shape[0]
    ring = _RowRing(dest_hbm, yb_hbm, dest_smem, ybuf, isem, gsem)

    @pl.when(i == 0)
    def _():
        ring.prologue()

    @pl.when(i + 2 < nt)
    def _():
        ring.idx_copy(i + 2).start()

    @pl.when(i + 1 < nt)
    def _():
        ring.idx_copy(i + 1).wait()
        ring.issue_rows(i + 1, unrolled=True)

    ring.wait_rows(i)
    slot = i % 2
    acc = h_ref[...]
    gates = gate_ref[...]
    for kk in range(TOP_K):
        acc = acc + gates[:, kk:kk + 1] * ybuf[slot, kk * tm:(kk + 1) * tm, :]
    o_ref[...] = _rms(acc, nw_ref[...])


def _combine(dest_tiles, yb, h, gates_t, nfw):
    t, d = h.shape
    nt, n = dest_tiles.shape
    tm = n // TOP_K
    row = lambda i: (i, 0)
    return pl.pallas_call(
        _combine_body,
        grid=(nt,),
        in_specs=[pl.BlockSpec(memory_space=pl.ANY), pl.BlockSpec(memory_space=pl.ANY),
                  pl.BlockSpec((tm, d), row), pl.BlockSpec((tm, TOP_K), row),
                  pl.BlockSpec((1, d), lambda i: (0, 0))],
        out_specs=pl.BlockSpec((tm, d), row),
        out_shape=jax.ShapeDtypeStruct((t, d), F32),
        scratch_shapes=[pltpu.VMEM((2, n, d), F32), pltpu.SMEM((2, n), I32),
                        pltpu.SemaphoreType.DMA((2,)), pltpu.SemaphoreType.DMA((2,))],
        compiler_params=_params(("arbitrary",)),
        name="combine",
    )(dest_tiles, yb, h, gates_t, nfw)


def _head_ones():
    hi = lax.broadcasted_iota(I32, (RWKV_WIDTH, RWKV_WIDTH), 0) // HEAD
    hj = lax.broadcasted_iota(I32, (RWKV_WIDTH, RWKV_WIDTH), 1) // HEAD
    return (hi == hj).astype(BF16)


def _layer(x2, bsz, seq, norm_mix_w, w_in, conv_w, mu_shift, w0, w_decay_up, a0, w_iclr_up,
           w_gate_up, k_k, k_a, r_k, ln_x_w, ln_x_b, w_out, norm_ffn_w, router_w, router_b,
           gu_w, gu_b, dn_w, dn_b):
    t, d = x2.shape
    row1 = lambda a: a.reshape(1, -1)

    p = _inproj(x2, row1(norm_mix_w), w_in.astype(BF16))
    yconv, r, k, v, lw, a, b, g, bonus = _prep(
        p, bsz, seq, conv_w, row1(mu_shift), row1(w0), w_decay_up, row1(a0), w_iclr_up,
        w_gate_up, row1(k_k), row1(k_a), row1(r_k), _head_ones())
    yn = _scan(r, k, v, lw, a, b, bsz, seq)
    h, xn2, idx, gates, rank, cnt = _mixout(
        yn, bonus, g, yconv, x2, row1(ln_x_w), row1(ln_x_b), w_out.astype(BF16),
        row1(norm_ffn_w), router_w.T, router_b.reshape(-1, 1))

    bm = ROW_BLOCK
    counts = cnt[:, 0].astype(I32)
    pcounts = (counts + bm - 1) // bm * bm
    pends = jnp.cumsum(pcounts)
    pstarts = pends - pcounts
    eids = jnp.arange(N_EXPERTS, dtype=I32)
    dest = rank + jnp.sum(jnp.where(idx[None] == eids[:, None, None], pstarts[:, None, None], 0),
                          axis=0)
    n_rows = t * TOP_K + N_EXPERTS * bm
    n_blocks = n_rows // bm
    tok_of_pair = jnp.tile(jnp.arange(t, dtype=I32), TOP_K)
    row_tok = jnp.zeros((n_rows,), I32).at[dest.reshape(-1)].set(tok_of_pair, unique_indices=True)
    blk_start = jnp.arange(n_blocks, dtype=I32) * bm
    blk_e = jnp.minimum(jnp.sum((pends[None, :] <= blk_start[:, None]).astype(I32), axis=1),
                        N_EXPERTS - 1)
    nused = (pends[-1] // bm).astype(I32).reshape(1)

    grp = 2 * LANES
    f2 = gu_w.shape[2]
    src = jnp.arange(grp, dtype=I32)[:, None]
    dst = jnp.arange(grp, dtype=I32)[None, :]
    perm = (src == jnp.where(dst < LANES, 2 * dst, 2 * (dst - LANES) + 1)).astype(BF16)
    gub = gu_b.reshape(N_EXPERTS, f2 // grp, LANES, 2).transpose(0, 1, 3, 2).reshape(N_EXPERTS, 1, f2)
    yb = _experts(blk_e, nused, row_tok.reshape(n_blocks, bm), xn2, gu_w, gub, dn_w,
                  dn_b.reshape(N_EXPERTS, 1, d), perm)

    tm = min(t, 256)
    nt = t // tm
    assert nt >= 2
    dest_tiles = dest.reshape(TOP_K, nt, tm).transpose(1, 0, 2).reshape(nt, TOP_K * tm)
    return dest_tiles, yb, h, gates.T


def kernel(x, norm_mix_w, w_in, conv_w, mu_shift, w0, w_decay_up, a0, w_iclr_up, w_gate_up,
           k_k, k_a, r_k, ln_x_w, ln_x_b, w_out, norm_ffn_w, router_w, router_b,
           gu_w, gu_b, dn_w, dn_b, norm_final_w):
    bsz, seq, d = x.shape
    assert w_in.shape[0] == 1, "the final RMSNorm is fused into the single layer's combine step"
    dest_tiles, yb, h_mid, gates_t = _layer(
        x.reshape(bsz * seq, d), bsz, seq, norm_mix_w[0], w_in[0], conv_w[0], mu_shift[0], w0[0],
        w_decay_up[0], a0[0], w_iclr_up[0], w_gate_up[0], k_k[0], k_a[0], r_k[0], ln_x_w[0],
        ln_x_b[0], w_out[0], norm_ffn_w[0], router_w[0], router_b[0], gu_w[0], gu_b[0], dn_w[0],
        dn_b[0])
    out = _combine(dest_tiles, yb, h_mid, gates_t, norm_final_w.reshape(1, d))
    return out.reshape(bsz, seq, d)
```

```python
import functools

import jax
import jax.numpy as jnp
from jax import lax
from jax.experimental import pallas as pl
from jax.experimental.pallas import tpu as pltpu

F32 = jnp.float32
BF16 = jnp.bfloat16
I32 = jnp.int32

CONV_WIDTH = 512
CONV_K = 3
RWKV_WIDTH = 512
HEAD = 64
HEADS = RWKV_WIDTH // HEAD
LORA_DECAY = 64
LORA_ICLR = 64
LORA_GATE = 128
N_EXPERTS = 32
TOP_K = 4
SWIGLU_LIMIT = 7.0
SWIGLU_ALPHA = 1.702
RMS_EPS = 1e-5
GN_EPS = 64e-5

CHUNK = 64
SUB = 16
ROW_BLOCK = 256
LANES = 128
VMEM_LIMIT = 56 * 1024 * 1024


def _params(sem, vmem=VMEM_LIMIT):
    return pltpu.CompilerParams(dimension_semantics=sem, vmem_limit_bytes=vmem)


def _mm(a, b):
    return jnp.dot(a.astype(BF16), b.astype(BF16), preferred_element_type=F32)


def _mm_nt(a, b):
    return lax.dot_general(a.astype(BF16), b.astype(BF16), (((1,), (1,)), ((), ())),
                           preferred_element_type=F32)


def _mm_tn(a, b):
    return lax.dot_general(a.astype(BF16), b.astype(BF16), (((0,), (0,)), ((), ())),
                           preferred_element_type=F32)


def _split_dot(x, m01, terms, left):
    acc = None
    rem = x
    for _ in range(terms):
        piece = rem.astype(BF16)
        rem = rem - piece.astype(F32)
        part = (jnp.dot(m01, piece, preferred_element_type=F32) if left
                else jnp.dot(piece, m01, preferred_element_type=F32))
        acc = part if acc is None else acc + part
    return acc


def _sigmoid(x):
    return 1.0 / (1.0 + jnp.exp(-x))


def _rms(x, w):
    return x * lax.rsqrt(jnp.mean(x * x, axis=-1, keepdims=True) + RMS_EPS) * w


def _store_row_tiles(ref, x):
    n, width = x.shape
    c = width // LANES
    for j in range(c):
        ref[pl.ds(j, n, stride=c), :] = x[:, j * LANES:(j + 1) * LANES]


def _load_row_tiles(ref, n, c, first=0):
    return jnp.concatenate([ref[pl.ds(first + j, n, stride=c), :] for j in range(c)], axis=1)


def _inproj_body(x_ref, nw_ref, w_ref, p_ref):
    xn = _rms(x_ref[...], nw_ref[...])
    p_ref[...] = jnp.dot(xn.astype(BF16), w_ref[...], preferred_element_type=F32)


def _inproj(x2, nw, w_bf):
    t, d = x2.shape
    c = w_bf.shape[1]
    tm = min(t, 512)
    return pl.pallas_call(
        _inproj_body,
        grid=(t // tm,),
        in_specs=[pl.BlockSpec((tm, d), lambda i: (i, 0)),
                  pl.BlockSpec((1, d), lambda i: (0, 0)),
                  pl.BlockSpec((d, c), lambda i: (0, 0))],
        out_specs=pl.BlockSpec((tm, c), lambda i: (i, 0)),
        out_shape=jax.ShapeDtypeStruct((t, c), F32),
        compiler_params=_params(("parallel",)),
        name="inproj",
    )(x2, nw, w_bf)


def _prep_body(p_ref, convw_ref, mu_ref, w0_ref, wdu_ref, a0_ref, wiu_ref, wgu_ref,
               kk_ref, ka_ref, rk_ref, ones_ref,
               yconv_ref, r_ref, k_ref, v_ref, lw_ref, a_ref, b_ref, g_ref, bonus_ref,
               buf_ref):
    ts = p_ref.shape[0]
    cw, rw = CONV_WIDTH, RWKV_WIDTH
    o_r = 3 * cw

    @pl.when(pl.program_id(1) == 0)
    def _():
        buf_ref[0:8, :] = jnp.zeros((8, buf_ref.shape[1]), F32)

    buf_ref[8:8 + ts, :] = p_ref[...]

    u0 = buf_ref[8:8 + ts, cw:2 * cw] * buf_ref[8:8 + ts, 2 * cw:3 * cw]
    u1 = buf_ref[7:7 + ts, cw:2 * cw] * buf_ref[7:7 + ts, 2 * cw:3 * cw]
    u2 = buf_ref[6:6 + ts, cw:2 * cw] * buf_ref[6:6 + ts, 2 * cw:3 * cw]
    conv = convw_ref[0:1, :] * u2 + convw_ref[1:2, :] * u1 + convw_ref[2:3, :] * u0
    yconv_ref[...] = buf_ref[8:8 + ts, 0:cw] * conv

    def shifted(lo, hi):
        cur = buf_ref[8:8 + ts, o_r + lo:o_r + hi]
        prev = buf_ref[7:7 + ts, o_r + lo:o_r + hi]
        return cur + (prev - cur) * mu_ref[:, lo:hi]

    r = shifted(0, rw)
    k = shifted(rw, 2 * rw)
    v = shifted(2 * rw, 3 * rw)
    o1 = 3 * rw
    d_w = shifted(o1, o1 + LORA_DECAY)
    d_a = shifted(o1 + LORA_DECAY, o1 + LORA_DECAY + LORA_ICLR)
    d_g = shifted(o1 + LORA_DECAY + LORA_ICLR, o1 + LORA_DECAY + LORA_ICLR + LORA_GATE)

    z = -(w0_ref[...] + _mm(jnp.tanh(d_w), wdu_ref[...]))
    softplus = jnp.maximum(z, 0.0) + jnp.log(1.0 + jnp.exp(-jnp.abs(z)))
    w_log = -softplus - 0.5
    lw_ref[...] = -jnp.exp(w_log)
    iclr = _sigmoid(a0_ref[...] + _mm(d_a, wiu_ref[...]))
    g_ref[...] = _mm(_sigmoid(d_g), wgu_ref[...])

    ones = ones_ref[...]
    kk = k * kk_ref[...]
    nrm = jnp.sqrt(_split_dot(kk * kk, ones, 2, left=False))
    kk = kk / jnp.maximum(nrm, 1e-12)
    k2 = k * (1.0 + (iclr - 1.0) * ka_ref[...])
    r_ref[...] = r
    k_ref[...] = k2
    v_ref[...] = v
    a_ref[...] = -kk
    b_ref[...] = kk * iclr
    bonus_ref[...] = _split_dot(r * k2 * rk_ref[...], ones, 2, left=False) * v

    buf_ref[0:8, :] = buf_ref[ts:ts + 8, :]


def _prep(p, bsz, seq, convw, mu, w0, wdu, a0, wiu, wgu, kkw, kaw, rkw, ones):
    t, c = p.shape
    ts = min(seq, 256)
    ns = seq // ts
    row = lambda b, s: (b * ns + s, 0)
    const = lambda b, s: (0, 0)
    full = lambda arr: pl.BlockSpec(arr.shape, const)
    out_spec = pl.BlockSpec((ts, RWKV_WIDTH), row)
    out_sds = jax.ShapeDtypeStruct((t, RWKV_WIDTH), F32)
    return pl.pallas_call(
        _prep_body,
        grid=(bsz, ns),
        in_specs=[pl.BlockSpec((ts, c), row)] + [full(a) for a in
                  (convw, mu, w0, wdu, a0, wiu, wgu, kkw, kaw, rkw, ones)],
        out_specs=[out_spec] * 9,
        out_shape=[out_sds] * 9,
        scratch_shapes=[pltpu.VMEM((ts + 8, c), F32)],
        compiler_params=_params(("parallel", "arbitrary")),
        name="prep",
    )(p, convw, mu, w0, wdu, a0, wiu, wgu, kkw, kaw, rkw, ones)


def _unit_lower_inverse(p_low, eye, bd_mask):
    pd = [jnp.where(bd_mask, p, 0.0) for p in p_low]
    po = [p - x for p, x in zip(p_low, pd)]
    d = [eye + x for x in pd]
    pk = pd
    for _ in range(3):
        pk = [_mm(x, x) for x in pk]
        dd = [_mm(x, y) for x, y in zip(d, pk)]
        d = [x + y for x, y in zip(d, dd)]
    q = [_mm(x, y) for x, y in zip(d, po)]
    q2 = [_mm(x, x) for x in q]
    q3 = [_mm(x, y) for x, y in zip(q, q2)]
    w = [eye + a + b + c for a, b, c in zip(q, q2, q3)]
    return [_mm(x, y) for x, y in zip(w, d)]


def _scan_body(r_ref, k_ref, v_ref, lw_ref, a_ref, b_ref, y_ref, s_ref):
    L = CHUNK

    @pl.when(pl.program_id(1) == 0)
    def _():
        s_ref[...] = jnp.zeros(s_ref.shape, F32)

    row = lax.broadcasted_iota(I32, (L, L), 0)
    col = lax.broadcasted_iota(I32, (L, L), 1)
    lower = row >= col
    strict = row > col
    eye = (row == col).astype(F32)
    bd_mask = (row // SUB) == (col // SUB)

    lw = lw_ref[...]
    c = _split_dot(lw, lower.astype(BF16), 3, left=True)
    e_in = jnp.exp(c)
    e_inv = jnp.exp(-c)
    e_ex = jnp.exp(c - lw)
    g_last = e_in[L - 1:L, :]
    a_t = a_ref[...] * e_ex
    r_t = r_ref[...] * e_in
    b_t = b_ref[...] * e_inv
    k_t = k_ref[...] * e_inv
    b_h = b_t * g_last
    k_h = k_t * g_last
    v = v_ref[...]

    hs = range(HEADS)
    sls = [slice(h * HEAD, (h + 1) * HEAD) for h in hs]
    ar = [jnp.concatenate([a_t[:, s], r_t[:, s]], axis=0) for s in sls]
    s_old = [s_ref[h] for h in hs]
    ars = [_mm_nt(ar[h], s_old[h]) for h in hs]
    gb = [_mm_nt(ar[h], b_t[:, sls[h]]) for h in hs]
    gk = [_mm_nt(ar[h], k_t[:, sls[h]]) for h in hs]
    p_low = [jnp.where(strict, x[0:L], 0.0) for x in gb]
    m_br = [jnp.where(lower, x[L:2 * L], 0.0) for x in gb]
    m_akr = [jnp.concatenate([jnp.where(strict, x[0:L], 0.0), jnp.where(lower, x[L:2 * L], 0.0)],
                             axis=0) for x in gk]
    vh = [v[:, s] for s in sls]
    mv = [_mm(m_akr[h], vh[h]) for h in hs]
    t_inv = _unit_lower_inverse(p_low, eye, bd_mask)
    u = [_mm(t_inv[h], ars[h][0:L] + mv[h][0:L]) for h in hs]
    mu = [_mm(m_br[h], u[h]) for h in hs]
    s_add = [_mm_tn(jnp.concatenate([u[h], vh[h]], axis=0),
                    jnp.concatenate([b_h[:, sls[h]], k_h[:, sls[h]]], axis=0)) for h in hs]
    for h in hs:
        s_ref[h] = s_old[h] * g_last[:, sls[h]] + s_add[h]
        y = ars[h][L:2 * L] + mu[h] + mv[h][L:2 * L]
        mean = jnp.mean(y, axis=-1, keepdims=True)
        yc = y - mean
        var = jnp.mean(yc * yc, axis=-1, keepdims=True)
        y_ref[:, sls[h]] = yc * lax.rsqrt(var + GN_EPS)


def _scan(r, k, v, lw, a, b, bsz, seq):
    t = r.shape[0]
    nc = seq // CHUNK
    spec = pl.BlockSpec((CHUNK, RWKV_WIDTH), lambda bi, ci: (bi * nc + ci, 0))
    return pl.pallas_call(
        _scan_body,
        grid=(bsz, nc),
        in_specs=[spec] * 6,
        out_specs=spec,
        out_shape=jax.ShapeDtypeStruct((t, RWKV_WIDTH), F32),
        scratch_shapes=[pltpu.VMEM((HEADS, HEAD, HEAD), F32)],
        compiler_params=_params(("parallel", "arbitrary")),
        name="scan",
    )(r, k, v, lw, a, b)


def _mixout_body(yn_ref, bonus_ref, g_ref, yconv_ref, x_ref, lnw_ref, lnb_ref, wout_ref,
                 nfw_ref, rwt_ref, rb_ref,
                 h_ref, xn_ref, idx_ref, gate_ref, rank_ref, cnt_ref, carry_ref):
    tm = x_ref.shape[0]

    @pl.when(pl.program_id(0) == 0)
    def _():
        carry_ref[...] = jnp.zeros(carry_ref.shape, F32)

    y_rwkv = (yn_ref[...] * lnw_ref[...] + lnb_ref[...] + bonus_ref[...]) * g_ref[...]
    cat = jnp.concatenate([yconv_ref[...], y_rwkv], axis=1)
    h = x_ref[...] + jnp.dot(cat.astype(BF16), wout_ref[...], preferred_element_type=F32)
    h_ref[...] = h
    xn = _rms(h, nfw_ref[...])
    _store_row_tiles(xn_ref, xn)

    lt = lax.dot_general(rwt_ref[...], xn, (((1,), (1,)), ((), ())),
                         precision=lax.Precision.HIGHEST, preferred_element_type=F32)
    lt = lt + rb_ref[...]
    eio = lax.broadcasted_iota(I32, lt.shape, 0)
    vals, onehots = [], []
    cur = lt
    for kk in range(TOP_K):
        m = jnp.max(cur, axis=0, keepdims=True)
        ix = jnp.min(jnp.where(cur == m, eio, N_EXPERTS), axis=0, keepdims=True)
        sel = eio == ix
        idx_ref[kk:kk + 1, :] = ix
        vals.append(m)
        onehots.append(sel.astype(F32))
        cur = jnp.where(sel, -jnp.inf, cur)
    exps = [jnp.exp(vv - vals[0]) for vv in vals]
    den = exps[0] + exps[1] + exps[2] + exps[3]
    for kk in range(TOP_K):
        gate_ref[kk:kk + 1, :] = exps[kk] / den

    trow = lax.broadcasted_iota(I32, (tm, tm), 0)
    tcol = lax.broadcasted_iota(I32, (tm, tm), 1)
    tri = (trow <= tcol).astype(BF16)
    ohs = jnp.concatenate(onehots, axis=0).astype(BF16)
    pref = jnp.dot(ohs, tri, preferred_element_type=F32)
    base = carry_ref[...]
    for kk in range(TOP_K):
        pk = pref[kk * N_EXPERTS:(kk + 1) * N_EXPERTS]
        rk = jnp.sum(onehots[kk] * (pk - 1.0 + base), axis=0, keepdims=True)
        rank_ref[kk:kk + 1, :] = rk.astype(I32)
        base = base + jnp.sum(onehots[kk], axis=1, keepdims=True)
    carry_ref[...] = base
    cnt_ref[...] = base


def _mixout(yn, bonus, g, yconv, x2, lnw, lnb, wout_bf, nfw, rwt, rb):
    t, d = x2.shape
    tm = min(t, 512)
    row = lambda i: (i, 0)
    const = lambda i: (0, 0)
    colb = lambda i: (0, i)
    half = pl.BlockSpec((tm, RWKV_WIDTH), row)
    full = lambda arr: pl.BlockSpec(arr.shape, const)
    return pl.pallas_call(
        _mixout_body,
        grid=(t // tm,),
        in_specs=[half, half, half, half, pl.BlockSpec((tm, d), row),
                  full(lnw), full(lnb), full(wout_bf), full(nfw), full(rwt), full(rb)],
        out_specs=[pl.BlockSpec((tm, d), row), pl.BlockSpec((tm * (d // LANES), LANES), row),
                   pl.BlockSpec((TOP_K, tm), colb), pl.BlockSpec((TOP_K, tm), colb),
                   pl.BlockSpec((TOP_K, tm), colb), pl.BlockSpec((N_EXPERTS, 1), const)],
        out_shape=[jax.ShapeDtypeStruct((t, d), F32),
                   jax.ShapeDtypeStruct((t * (d // LANES), LANES), F32),
                   jax.ShapeDtypeStruct((TOP_K, t), I32), jax.ShapeDtypeStruct((TOP_K, t), F32),
                   jax.ShapeDtypeStruct((TOP_K, t), I32),
                   jax.ShapeDtypeStruct((N_EXPERTS, 1), F32)],
        scratch_shapes=[pltpu.VMEM((N_EXPERTS, 1), F32)],
        compiler_params=_params(("arbitrary",)),
        name="mix_out",
    )(yn, bonus, g, yconv, x2, lnw, lnb, wout_bf, nfw, rwt, rb)


class _RowRing:
    def __init__(self, idx_hbm, src_hbm, idx_smem, rows, isem, gsem):
        self.idx_hbm, self.src_hbm, self.idx_smem, self.rows = idx_hbm, src_hbm, idx_smem, rows
        self.isem, self.gsem = isem, gsem
        self.n = idx_smem.shape[1]
        self.c = rows.shape[1] // self.n

    def idx_copy(self, blk):
        return pltpu.make_async_copy(self.idx_hbm.at[blk], self.idx_smem.at[blk % 2],
                                     self.isem.at[blk % 2])

    def row_copy(self, slot, src_row, r):
        c = self.c
        first = src_row * c if isinstance(src_row, int) else pl.multiple_of(src_row * c, c)
        return pltpu.make_async_copy(self.src_hbm.at[pl.ds(first, c)],
                                     self.rows.at[slot, pl.ds(r * c, c)], self.gsem.at[slot])

    def issue_rows(self, blk, unrolled):
        slot = blk % 2
        if unrolled:
            for r in range(self.n):
                self.row_copy(slot, self.idx_smem[slot, r], r).start()
        else:
            def body(r, carry):
                self.row_copy(slot, self.idx_smem[slot, r], r).start()
                return carry
            lax.fori_loop(0, self.n, body, 0, unroll=8)

    def wait_rows(self, blk):
        slot = blk % 2
        for r in range(self.n):
            self.row_copy(slot, 0, r).wait()

    def prologue(self):
        first = self.idx_copy(0)
        first.start()
        first.wait()
        self.issue_rows(0, unrolled=False)
        self.idx_copy(1).start()


def _expert_body(blk_e_ref, nused_ref, tok_hbm, xn_hbm, gu_ref, gub_ref, dn_ref, dnb_ref, perm_ref,
                 y_ref, xbuf, wgl, wd, tok_smem, isem, gsem):
    i = pl.program_id(0)
    n_live = nused_ref[0]
    ring = _RowRing(tok_hbm, xn_hbm, tok_smem, xbuf, isem, gsem)
    grp = perm_ref.shape[0]
    half = grp // 2

    @pl.when(i == 0)
    def _():
        ring.prologue()

    @pl.when(i + 2 <= n_live)
    def _():
        ring.idx_copy(i + 2).start()

    new_expert = (i == 0) | (blk_e_ref[i] != blk_e_ref[jnp.maximum(i - 1, 0)])

    @pl.when((i < n_live) & new_expert)
    def _():
        for c in range(gu_ref.shape[1] // grp):
            cols = slice(c * grp, (c + 1) * grp)
            wgl[:, cols] = jnp.dot(gu_ref[:, cols].astype(BF16), perm_ref[...],
                                   preferred_element_type=F32).astype(BF16)
        wd[...] = dn_ref[...].astype(BF16)

    @pl.when(i < n_live)
    def _():
        ring.idx_copy(i + 1).wait()
        ring.issue_rows(i + 1, unrolled=True)
        ring.wait_rows(i)
        xb = _load_row_tiles(xbuf.at[i % 2], ring.n, ring.c).astype(BF16)
        h = jnp.dot(xb, wgl[...], preferred_element_type=F32) + gub_ref[...]
        acts = []
        for c in range(h.shape[1] // grp):
            x_glu = jnp.minimum(h[:, c * grp:c * grp + half], SWIGLU_LIMIT)
            x_lin = jnp.clip(h[:, c * grp + half:(c + 1) * grp], -SWIGLU_LIMIT, SWIGLU_LIMIT)
            acts.append(x_glu * _sigmoid(SWIGLU_ALPHA * x_glu) * (x_lin + 1.0))
        act = jnp.concatenate(acts, axis=1)
        y = jnp.dot(act.astype(BF16), wd[...], preferred_element_type=F32) + dnb_ref[...]
        _store_row_tiles(y_ref, y)

    @pl.when(i == n_live)
    def _():
        ring.wait_rows(i)

    @pl.when(i >= n_live)
    def _():
        y_ref[...] = jnp.zeros(y_ref.shape, F32)


def _experts(blk_e, nused, row_tok, xn2, gu_w, gu_b, dn_w, dn_b, perm):
    nb, bm = row_tok.shape
    f2 = gu_w.shape[2]
    f, d = dn_w.shape[1:]
    c = d // LANES
    wspec = lambda shape: pl.BlockSpec((None,) + shape, lambda i, be, nu: (be[i], 0, 0))
    grid_spec = pltpu.PrefetchScalarGridSpec(
        num_scalar_prefetch=2,
        grid=(nb,),
        in_specs=[pl.BlockSpec(memory_space=pl.ANY), pl.BlockSpec(memory_space=pl.ANY),
                  wspec((d, f2)), wspec((1, f2)), wspec((f, d)), wspec((1, d)),
                  pl.BlockSpec(perm.shape, lambda i, be, nu: (0, 0))],
        out_specs=pl.BlockSpec((bm * c, LANES), lambda i, be, nu: (i, 0)),
        scratch_shapes=[pltpu.VMEM((2, bm * c, LANES), F32), pltpu.VMEM((d, f2), BF16),
                        pltpu.VMEM((f, d), BF16), pltpu.SMEM((2, bm), I32),
                        pltpu.SemaphoreType.DMA((2,)), pltpu.SemaphoreType.DMA((2,))],
    )
    return pl.pallas_call(
        _expert_body,
        grid_spec=grid_spec,
        out_shape=jax.ShapeDtypeStruct((nb * bm * c, LANES), F32),
        compiler_params=_params(("arbitrary",)),
        name="experts",
    )(blk_e, nused, row_tok, xn2, gu_w, gu_b, dn_w, dn_b, perm)


def _combine_body(dest_hbm, yb_hbm, h_ref, gate_ref, nw_ref, o_ref, ybuf, dest_smem, isem, gsem):
    i = pl.program_id(0)
    nt = pl.num_programs(0)
    tm = h_ref.shape[0]
    ring = _RowRing(dest_hbm, yb_hbm, dest_smem, ybuf, isem, gsem)

    @pl.when(i == 0)
    def _():
        ring.prologue()

    @pl.when(i + 2 < nt)
    def _():
        ring.idx_copy(i + 2).start()

    @pl.when(i + 1 < nt)
    def _():
        ring.idx_copy(i + 1).wait()
        ring.issue_rows(i + 1, unrolled=True)

    ring.wait_rows(i)
    rows = ybuf.at[i % 2]
    c = ring.c
    gates = gate_ref[...]
    acc = h_ref[...]
    for kk in range(TOP_K):
        acc = acc + gates[:, kk:kk + 1] * _load_row_tiles(rows, tm, c, first=kk * tm * c)
    o_ref[...] = _rms(acc, nw_ref[...])


def _combine(dest_tiles, yb, h, gates_t, nfw):
    t, d = h.shape
    nt, n = dest_tiles.shape
    tm = n // TOP_K
    row = lambda i: (i, 0)
    return pl.pallas_call(
        _combine_body,
        grid=(nt,),
        in_specs=[pl.BlockSpec(memory_space=pl.ANY), pl.BlockSpec(memory_space=pl.ANY),
                  pl.BlockSpec((tm, d), row), pl.BlockSpec((tm, TOP_K), row),
                  pl.BlockSpec((1, d), lambda i: (0, 0))],
        out_specs=pl.BlockSpec((tm, d), row),
        out_shape=jax.ShapeDtypeStruct((t, d), F32),
        scratch_shapes=[pltpu.VMEM((2, n * (d // LANES), LANES), F32), pltpu.SMEM((2, n), I32),
                        pltpu.SemaphoreType.DMA((2,)), pltpu.SemaphoreType.DMA((2,))],
        compiler_params=_params(("arbitrary",)),
        name="combine",
    )(dest_tiles, yb, h, gates_t, nfw)


def _head_ones():
    hi = lax.broadcasted_iota(I32, (RWKV_WIDTH, RWKV_WIDTH), 0) // HEAD
    hj = lax.broadcasted_iota(I32, (RWKV_WIDTH, RWKV_WIDTH), 1) // HEAD
    return (hi == hj).astype(BF16)


def _layer(x2, bsz, seq, norm_mix_w, w_in, conv_w, mu_shift, w0, w_decay_up, a0, w_iclr_up,
           w_gate_up, k_k, k_a, r_k, ln_x_w, ln_x_b, w_out, norm_ffn_w, router_w, router_b,
           gu_w, gu_b, dn_w, dn_b):
    t, d = x2.shape
    row1 = lambda a: a.reshape(1, -1)

    p = _inproj(x2, row1(norm_mix_w), w_in.astype(BF16))
    yconv, r, k, v, lw, a, b, g, bonus = _prep(
        p, bsz, seq, conv_w, row1(mu_shift), row1(w0), w_decay_up, row1(a0), w_iclr_up,
        w_gate_up, row1(k_k), row1(k_a), row1(r_k), _head_ones())
    yn = _scan(r, k, v, lw, a, b, bsz, seq)
    h, xn2, idx, gates, rank, cnt = _mixout(
        yn, bonus, g, yconv, x2, row1(ln_x_w), row1(ln_x_b), w_out.astype(BF16),
        row1(norm_ffn_w), router_w.T, router_b.reshape(-1, 1))

    bm = ROW_BLOCK
    counts = cnt[:, 0].astype(I32)
    pcounts = (counts + bm - 1) // bm * bm
    pends = jnp.cumsum(pcounts)
    pstarts = pends - pcounts
    eids = jnp.arange(N_EXPERTS, dtype=I32)
    dest = rank + jnp.sum(jnp.where(idx[None] == eids[:, None, None], pstarts[:, None, None], 0),
                          axis=0)
    n_rows = t * TOP_K + N_EXPERTS * bm
    n_blocks = n_rows // bm
    tok_of_pair = jnp.tile(jnp.arange(t, dtype=I32), TOP_K)
    row_tok = jnp.zeros((n_rows,), I32).at[dest.reshape(-1)].set(tok_of_pair, unique_indices=True)
    blk_start = jnp.arange(n_blocks, dtype=I32) * bm
    blk_e = jnp.minimum(jnp.sum((pends[None, :] <= blk_start[:, None]).astype(I32), axis=1),
                        N_EXPERTS - 1)
    nused = (pends[-1] // bm).astype(I32).reshape(1)

    grp = 2 * LANES
    f2 = gu_w.shape[2]
    src = jnp.arange(grp, dtype=I32)[:, None]
    dst = jnp.arange(grp, dtype=I32)[None, :]
    perm = (src == jnp.where(dst < LANES, 2 * dst, 2 * (dst - LANES) + 1)).astype(BF16)
    gub = gu_b.reshape(N_EXPERTS, f2 // grp, LANES, 2).transpose(0, 1, 3, 2).reshape(N_EXPERTS, 1, f2)
    yb = _experts(blk_e, nused, row_tok.reshape(n_blocks, bm), xn2, gu_w, gub, dn_w,
                  dn_b.reshape(N_EXPERTS, 1, d), perm)

    tm = min(t, 256)
    nt = t // tm
    assert nt >= 2
    dest_tiles = dest.reshape(TOP_K, nt, tm).transpose(1, 0, 2).reshape(nt, TOP_K * tm)
    return dest_tiles, yb, h, gates.T


def kernel(x, norm_mix_w, w_in, conv_w, mu_shift, w0, w_decay_up, a0, w_iclr_up, w_gate_up,
           k_k, k_a, r_k, ln_x_w, ln_x_b, w_out, norm_ffn_w, router_w, router_b,
           gu_w, gu_b, dn_w, dn_b, norm_final_w):
    bsz, seq, d = x.shape
    assert w_in.shape[0] == 1, "the final RMSNorm is fused into the single layer's combine step"
    dest_tiles, yb, h_mid, gates_t = _layer(
        x.reshape(bsz * seq, d), bsz, seq, norm_mix_w[0], w_in[0], conv_w[0], mu_shift[0], w0[0],
        w_decay_up[0], a0[0], w_iclr_up[0], w_gate_up[0], k_k[0], k_a[0], r_k[0], ln_x_w[0],
        ln_x_b[0], w_out[0], norm_ffn_w[0], router_w[0], router_b[0], gu_w[0], gu_b[0], dn_w[0],
        dn_b[0])
    out = _combine(dest_tiles, yb, h_mid, gates_t, norm_final_w.reshape(1, d))
    return out.reshape(bsz, seq, d)
```

```python
import functools

import jax
import jax.numpy as jnp
from jax import lax
from jax.experimental import pallas as pl
from jax.experimental.pallas import tpu as pltpu

F32 = jnp.float32
BF16 = jnp.bfloat16
I32 = jnp.int32

CONV_WIDTH = 512
CONV_K = 3
RWKV_WIDTH = 512
HEAD = 64
HEADS = RWKV_WIDTH // HEAD
LORA_DECAY = 64
LORA_ICLR = 64
LORA_GATE = 128
N_EXPERTS = 32
TOP_K = 4
SWIGLU_LIMIT = 7.0
SWIGLU_ALPHA = 1.702
RMS_EPS = 1e-5
GN_EPS = 64e-5

CHUNK = 64
SUB = 16
SCAN_CHUNKS = 2
ROW_BLOCK = 256
EXPERT_RING = 3
LANES = 128
VMEM_LIMIT = 56 * 1024 * 1024


def _params(sem, vmem=VMEM_LIMIT):
    return pltpu.CompilerParams(dimension_semantics=sem, vmem_limit_bytes=vmem)


def _mm(a, b):
    return jnp.dot(a.astype(BF16), b.astype(BF16), preferred_element_type=F32)


def _mm_nt(a, b):
    return lax.dot_general(a.astype(BF16), b.astype(BF16), (((1,), (1,)), ((), ())),
                           preferred_element_type=F32)


def _mm_tn(a, b):
    return lax.dot_general(a.astype(BF16), b.astype(BF16), (((0,), (0,)), ((), ())),
                           preferred_element_type=F32)


def _split_dot(x, m01, terms, left):
    acc = None
    rem = x
    for _ in range(terms):
        piece = rem.astype(BF16)
        rem = rem - piece.astype(F32)
        part = (jnp.dot(m01, piece, preferred_element_type=F32) if left
                else jnp.dot(piece, m01, preferred_element_type=F32))
        acc = part if acc is None else acc + part
    return acc


def _sigmoid(x):
    return 1.0 / (1.0 + jnp.exp(-x))


def _rms(x, w):
    return x * lax.rsqrt(jnp.mean(x * x, axis=-1, keepdims=True) + RMS_EPS) * w


def _store_row_tiles(ref, x):
    n, width = x.shape
    c = width // LANES
    for j in range(c):
        ref[pl.ds(j, n, stride=c), :] = x[:, j * LANES:(j + 1) * LANES]


def _load_row_tiles(ref, n, c, first=0):
    return jnp.concatenate([ref[pl.ds(first + j, n, stride=c), :] for j in range(c)], axis=1)


def _inproj_body(x_ref, nw_ref, w_ref, p_ref):
    xn = _rms(x_ref[...], nw_ref[...])
    p_ref[...] = jnp.dot(xn.astype(BF16), w_ref[...], preferred_element_type=F32)


def _inproj(x2, nw, w_bf):
    t, d = x2.shape
    c = w_bf.shape[1]
    tm = min(t, 512)
    return pl.pallas_call(
        _inproj_body,
        grid=(t // tm,),
        in_specs=[pl.BlockSpec((tm, d), lambda i: (i, 0)),
                  pl.BlockSpec((1, d), lambda i: (0, 0)),
                  pl.BlockSpec((d, c), lambda i: (0, 0))],
        out_specs=pl.BlockSpec((tm, c), lambda i: (i, 0)),
        out_shape=jax.ShapeDtypeStruct((t, c), F32),
        compiler_params=_params(("parallel",)),
        name="inproj",
    )(x2, nw, w_bf)


def _prep_body(p_ref, convw_ref, mu_ref, w0_ref, wdu_ref, a0_ref, wiu_ref, wgu_ref,
               kk_ref, ka_ref, rk_ref, ones_ref,
               yconv_ref, r_ref, k_ref, v_ref, lw_ref, a_ref, b_ref, g_ref, bonus_ref,
               buf_ref):
    ts = p_ref.shape[0]
    cw, rw = CONV_WIDTH, RWKV_WIDTH
    o_r = 3 * cw

    @pl.when(pl.program_id(1) == 0)
    def _():
        buf_ref[0:8, :] = jnp.zeros((8, buf_ref.shape[1]), F32)

    buf_ref[8:8 + ts, :] = p_ref[...]

    u0 = buf_ref[8:8 + ts, cw:2 * cw] * buf_ref[8:8 + ts, 2 * cw:3 * cw]
    u1 = buf_ref[7:7 + ts, cw:2 * cw] * buf_ref[7:7 + ts, 2 * cw:3 * cw]
    u2 = buf_ref[6:6 + ts, cw:2 * cw] * buf_ref[6:6 + ts, 2 * cw:3 * cw]
    conv = convw_ref[0:1, :] * u2 + convw_ref[1:2, :] * u1 + convw_ref[2:3, :] * u0
    yconv_ref[...] = buf_ref[8:8 + ts, 0:cw] * conv

    def shifted(lo, hi):
        cur = buf_ref[8:8 + ts, o_r + lo:o_r + hi]
        prev = buf_ref[7:7 + ts, o_r + lo:o_r + hi]
        return cur + (prev - cur) * mu_ref[:, lo:hi]

    r = shifted(0, rw)
    k = shifted(rw, 2 * rw)
    v = shifted(2 * rw, 3 * rw)
    o1 = 3 * rw
    d_w = shifted(o1, o1 + LORA_DECAY)
    d_a = shifted(o1 + LORA_DECAY, o1 + LORA_DECAY + LORA_ICLR)
    d_g = shifted(o1 + LORA_DECAY + LORA_ICLR, o1 + LORA_DECAY + LORA_ICLR + LORA_GATE)

    z = -(w0_ref[...] + _mm(jnp.tanh(d_w), wdu_ref[...]))
    softplus = jnp.maximum(z, 0.0) + jnp.log(1.0 + jnp.exp(-jnp.abs(z)))
    w_log = -softplus - 0.5
    lw_ref[...] = -jnp.exp(w_log)
    iclr = _sigmoid(a0_ref[...] + _mm(d_a, wiu_ref[...]))
    g_ref[...] = _mm(_sigmoid(d_g), wgu_ref[...])

    ones = ones_ref[...]
    kk = k * kk_ref[...]
    nrm = jnp.sqrt(_split_dot(kk * kk, ones, 2, left=False))
    kk = kk / jnp.maximum(nrm, 1e-12)
    k2 = k * (1.0 + (iclr - 1.0) * ka_ref[...])
    r_ref[...] = r
    k_ref[...] = k2
    v_ref[...] = v
    a_ref[...] = -kk
    b_ref[...] = kk * iclr
    bonus_ref[...] = _split_dot(r * k2 * rk_ref[...], ones, 2, left=False) * v

    buf_ref[0:8, :] = buf_ref[ts:ts + 8, :]


def _prep(p, bsz, seq, convw, mu, w0, wdu, a0, wiu, wgu, kkw, kaw, rkw, ones):
    t, c = p.shape
    ts = min(seq, 256)
    ns = seq // ts
    row = lambda b, s: (b * ns + s, 0)
    const = lambda b, s: (0, 0)
    full = lambda arr: pl.BlockSpec(arr.shape, const)
    out_spec = pl.BlockSpec((ts, RWKV_WIDTH), row)
    out_sds = jax.ShapeDtypeStruct((t, RWKV_WIDTH), F32)
    return pl.pallas_call(
        _prep_body,
        grid=(bsz, ns),
        in_specs=[pl.BlockSpec((ts, c), row)] + [full(a) for a in
                  (convw, mu, w0, wdu, a0, wiu, wgu, kkw, kaw, rkw, ones)],
        out_specs=[out_spec] * 9,
        out_shape=[out_sds] * 9,
        scratch_shapes=[pltpu.VMEM((ts + 8, c), F32)],
        compiler_params=_params(("parallel", "arbitrary")),
        name="prep",
    )(p, convw, mu, w0, wdu, a0, wiu, wgu, kkw, kaw, rkw, ones)


def _unit_lower_inverse(p_low, eye, bd_mask):
    pd = [jnp.where(bd_mask, p, 0.0) for p in p_low]
    po = [p - x for p, x in zip(p_low, pd)]
    d = [eye + x for x in pd]
    pk = pd
    for _ in range(3):
        pk = [_mm(x, x) for x in pk]
        dd = [_mm(x, y) for x, y in zip(d, pk)]
        d = [x + y for x, y in zip(d, dd)]
    q = [_mm(x, y) for x, y in zip(d, po)]
    q2 = [_mm(x, x) for x in q]
    q3 = [_mm(x, y) for x, y in zip(q, q2)]
    w = [eye + a + b + c for a, b, c in zip(q, q2, q3)]
    return [_mm(x, y) for x, y in zip(w, d)]


def _scan_body(r_ref, k_ref, v_ref, lw_ref, a_ref, b_ref, y_ref, s_ref):
    L = CHUNK

    @pl.when(pl.program_id(1) == 0)
    def _():
        s_ref[...] = jnp.zeros(s_ref.shape, F32)

    row = lax.broadcasted_iota(I32, (L, L), 0)
    col = lax.broadcasted_iota(I32, (L, L), 1)
    lower = row >= col
    strict = row > col
    eye = (row == col).astype(F32)
    bd_mask = (row // SUB) == (col // SUB)

    rows = lw_ref.shape[0]
    nsub = rows // L
    rr = lax.broadcasted_iota(I32, (rows, rows), 0)
    cc = lax.broadcasted_iota(I32, (rows, rows), 1)
    chunk_lower = ((rr >= cc) & (rr // L == cc // L)).astype(BF16)
    lw = lw_ref[...]
    c = _split_dot(lw, chunk_lower, 3, left=True)
    e_in = jnp.exp(c)
    e_inv = jnp.exp(-c)
    e_ex = jnp.exp(c - lw)
    a_t = a_ref[...] * e_ex
    r_t = r_ref[...] * e_in
    b_t = b_ref[...] * e_inv
    k_t = k_ref[...] * e_inv
    v = v_ref[...]

    hs = range(HEADS)
    sls = [slice(h * HEAD, (h + 1) * HEAD) for h in hs]
    items = [(slice(s * L, (s + 1) * L), sl) for s in range(nsub) for sl in sls]
    g_last = [e_in[(s + 1) * L - 1:(s + 1) * L, :] for s in range(nsub)]
    ar = [jnp.concatenate([a_t[ts, sl], r_t[ts, sl]], axis=0) for ts, sl in items]
    gb = [_mm_nt(x, b_t[ts, sl]) for x, (ts, sl) in zip(ar, items)]
    gk = [_mm_nt(x, k_t[ts, sl]) for x, (ts, sl) in zip(ar, items)]
    p_low = [jnp.where(strict, x[0:L], 0.0) for x in gb]
    m_br = [jnp.where(lower, x[L:2 * L], 0.0) for x in gb]
    m_akr = [jnp.concatenate([jnp.where(strict, x[0:L], 0.0), jnp.where(lower, x[L:2 * L], 0.0)],
                             axis=0) for x in gk]
    vh = [v[ts, sl] for ts, sl in items]
    mv = [_mm(x, y) for x, y in zip(m_akr, vh)]
    t_inv = _unit_lower_inverse(p_low, eye, bd_mask)

    state = [s_ref[h] for h in hs]
    for s in range(nsub):
        it = [s * HEADS + h for h in hs]
        ts = items[it[0]][0]
        ars = [_mm_nt(ar[j], state[h]) for h, j in zip(hs, it)]
        u = [_mm(t_inv[j], ars[h][0:L] + mv[j][0:L]) for h, j in zip(hs, it)]
        mu = [_mm(m_br[j], u[h]) for h, j in zip(hs, it)]
        s_add = [_mm_tn(jnp.concatenate([u[h], vh[j]], axis=0),
                        jnp.concatenate([b_t[ts, sls[h]], k_t[ts, sls[h]]], axis=0)
                        * g_last[s][:, sls[h]]) for h, j in zip(hs, it)]
        for h, j in zip(hs, it):
            state[h] = state[h] * g_last[s][:, sls[h]] + s_add[h]
            y = ars[h][L:2 * L] + mu[h] + mv[j][L:2 * L]
            mean = jnp.mean(y, axis=-1, keepdims=True)
            yc = y - mean
            var = jnp.mean(yc * yc, axis=-1, keepdims=True)
            y_ref[ts, sls[h]] = yc * lax.rsqrt(var + GN_EPS)
    for h in hs:
        s_ref[h] = state[h]


def _scan(r, k, v, lw, a, b, bsz, seq):
    t = r.shape[0]
    rows = CHUNK * SCAN_CHUNKS
    nc = seq // rows
    spec = pl.BlockSpec((rows, RWKV_WIDTH), lambda bi, ci: (bi * nc + ci, 0))
    return pl.pallas_call(
        _scan_body,
        grid=(bsz, nc),
        in_specs=[spec] * 6,
        out_specs=spec,
        out_shape=jax.ShapeDtypeStruct((t, RWKV_WIDTH), F32),
        scratch_shapes=[pltpu.VMEM((HEADS, HEAD, HEAD), F32)],
        compiler_params=_params(("parallel", "arbitrary")),
        name="scan",
    )(r, k, v, lw, a, b)


def _mixout_body(yn_ref, bonus_ref, g_ref, yconv_ref, x_ref, lnw_ref, lnb_ref, wout_ref,
                 nfw_ref, rwt_ref, rb_ref,
                 h_ref, xn_ref, idx_ref, gate_ref, rank_ref, cnt_ref, carry_ref):
    tm = x_ref.shape[0]

    @pl.when(pl.program_id(0) == 0)
    def _():
        carry_ref[...] = jnp.zeros(carry_ref.shape, F32)

    y_rwkv = (yn_ref[...] * lnw_ref[...] + lnb_ref[...] + bonus_ref[...]) * g_ref[...]
    cat = jnp.concatenate([yconv_ref[...], y_rwkv], axis=1)
    h = x_ref[...] + jnp.dot(cat.astype(BF16), wout_ref[...], preferred_element_type=F32)
    h_ref[...] = h
    xn = _rms(h, nfw_ref[...])
    _store_row_tiles(xn_ref, xn)

    lt = lax.dot_general(rwt_ref[...], xn, (((1,), (1,)), ((), ())),
                         precision=lax.Precision.HIGHEST, preferred_element_type=F32)
    lt = lt + rb_ref[...]
    eio = lax.broadcasted_iota(I32, lt.shape, 0)
    vals, onehots = [], []
    cur = lt
    for kk in range(TOP_K):
        m = jnp.max(cur, axis=0, keepdims=True)
        ix = jnp.min(jnp.where(cur == m, eio, N_EXPERTS), axis=0, keepdims=True)
        sel = eio == ix
        idx_ref[kk:kk + 1, :] = ix
        vals.append(m)
        onehots.append(sel.astype(F32))
        cur = jnp.where(sel, -jnp.inf, cur)
    exps = [jnp.exp(vv - vals[0]) for vv in vals]
    den = exps[0] + exps[1] + exps[2] + exps[3]
    for kk in range(TOP_K):
        gate_ref[kk:kk + 1, :] = exps[kk] / den

    trow = lax.broadcasted_iota(I32, (tm, tm), 0)
    tcol = lax.broadcasted_iota(I32, (tm, tm), 1)
    tri = (trow <= tcol).astype(BF16)
    ohs = jnp.concatenate(onehots, axis=0).astype(BF16)
    pref = jnp.dot(ohs, tri, preferred_element_type=F32)
    base = carry_ref[...]
    for kk in range(TOP_K):
        pk = pref[kk * N_EXPERTS:(kk + 1) * N_EXPERTS]
        rk = jnp.sum(onehots[kk] * (pk - 1.0 + base), axis=0, keepdims=True)
        rank_ref[kk:kk + 1, :] = rk.astype(I32)
        base = base + jnp.sum(onehots[kk], axis=1, keepdims=True)
    carry_ref[...] = base
    cnt_ref[...] = base


def _mixout(yn, bonus, g, yconv, x2, lnw, lnb, wout_bf, nfw, rwt, rb):
    t, d = x2.shape
    tm = min(t, 512)
    row = lambda i: (i, 0)
    const = lambda i: (0, 0)
    colb = lambda i: (0, i)
    half = pl.BlockSpec((tm, RWKV_WIDTH), row)
    full = lambda arr: pl.BlockSpec(arr.shape, const)
    return pl.pallas_call(
        _mixout_body,
        grid=(t // tm,),
        in_specs=[half, half, half, half, pl.BlockSpec((tm, d), row),
                  full(lnw), full(lnb), full(wout_bf), full(nfw), full(rwt), full(rb)],
        out_specs=[pl.BlockSpec((tm, d), row), pl.BlockSpec((tm * (d // LANES), LANES), row),
                   pl.BlockSpec((TOP_K, tm), colb), pl.BlockSpec((TOP_K, tm), colb),
                   pl.BlockSpec((TOP_K, tm), colb), pl.BlockSpec((N_EXPERTS, 1), const)],
        out_shape=[jax.ShapeDtypeStruct((t, d), F32),
                   jax.ShapeDtypeStruct((t * (d // LANES), LANES), F32),
                   jax.ShapeDtypeStruct((TOP_K, t), I32), jax.ShapeDtypeStruct((TOP_K, t), F32),
                   jax.ShapeDtypeStruct((TOP_K, t), I32),
                   jax.ShapeDtypeStruct((N_EXPERTS, 1), F32)],
        scratch_shapes=[pltpu.VMEM((N_EXPERTS, 1), F32)],
        compiler_params=_params(("arbitrary",)),
        name="mix_out",
    )(yn, bonus, g, yconv, x2, lnw, lnb, wout_bf, nfw, rwt, rb)


class _RowRing:
    def __init__(self, idx_hbm, src_hbm, idx_smem, rows, isem, gsem):
        self.idx_hbm, self.src_hbm, self.idx_smem, self.rows = idx_hbm, src_hbm, idx_smem, rows
        self.isem, self.gsem = isem, gsem
        self.depth, _, self.n = idx_smem.shape
        self.ahead = self.depth - 1
        self.c = rows.shape[1] // self.n

    def idx_copy(self, blk):
        slot = blk % self.depth
        return pltpu.make_async_copy(self.idx_hbm.at[blk], self.idx_smem.at[slot],
                                     self.isem.at[slot])

    def row_copy(self, slot, src_row, r):
        c = self.c
        first = src_row * c if isinstance(src_row, int) else pl.multiple_of(src_row * c, c)
        return pltpu.make_async_copy(self.src_hbm.at[pl.ds(first, c)],
                                     self.rows.at[slot, pl.ds(r * c, c)], self.gsem.at[slot])

    def issue_rows(self, blk, unrolled):
        slot = blk % self.depth
        if unrolled:
            for r in range(self.n):
                self.row_copy(slot, self.idx_smem[slot, 0, r], r).start(priority=r % 2)
        else:
            def body(r, carry):
                self.row_copy(slot, self.idx_smem[slot, 0, r], r).start()
                return carry
            lax.fori_loop(0, self.n, body, 0, unroll=8)

    def wait_rows(self, blk):
        slot = blk % self.depth
        for r in range(self.n):
            self.row_copy(slot, 0, r).wait()

    def prologue(self):
        for b in range(self.ahead):
            first = self.idx_copy(b)
            first.start()
            first.wait()
            self.issue_rows(b, unrolled=False)
        self.idx_copy(self.ahead).start()

    def start_next_idx(self, i, last):
        @pl.when(i + self.depth <= last)
        def _():
            self.idx_copy(i + self.depth).start()

    def request(self, i):
        self.idx_copy(i + self.ahead).wait()
        self.issue_rows(i + self.ahead, unrolled=True)


def _expert_body(blk_e_ref, nused_ref, tok_hbm, xn_hbm, gu_ref, gub_ref, dn_ref, dnb_ref, perm_ref,
                 y_ref, xbuf, wgl, wd, tok_smem, isem, gsem):
    i = pl.program_id(0)
    n_live = nused_ref[0]
    ring = _RowRing(tok_hbm, xn_hbm, tok_smem, xbuf, isem, gsem)
    grp = perm_ref.shape[0]
    half = grp // 2

    @pl.when(i == 0)
    def _():
        ring.prologue()

    last = n_live + ring.ahead - 1
    ring.start_next_idx(i, last)

    new_expert = (i == 0) | (blk_e_ref[i] != blk_e_ref[jnp.maximum(i - 1, 0)])

    @pl.when((i < n_live) & new_expert)
    def _():
        for c in range(gu_ref.shape[1] // grp):
            cols = slice(c * grp, (c + 1) * grp)
            wgl[:, cols] = jnp.dot(gu_ref[:, cols].astype(BF16), perm_ref[...],
                                   preferred_element_type=F32).astype(BF16)
        wd[...] = dn_ref[...].astype(BF16)

    @pl.when(i < n_live)
    def _():
        ring.request(i)
        ring.wait_rows(i)
        xb = _load_row_tiles(xbuf.at[i % ring.depth], ring.n, ring.c).astype(BF16)
        h = jnp.dot(xb, wgl[...], preferred_element_type=F32) + gub_ref[...]
        acts = []
        for c in range(h.shape[1] // grp):
            x_glu = jnp.minimum(h[:, c * grp:c * grp + half], SWIGLU_LIMIT)
            x_lin = jnp.clip(h[:, c * grp + half:(c + 1) * grp], -SWIGLU_LIMIT, SWIGLU_LIMIT)
            acts.append(x_glu * _sigmoid(SWIGLU_ALPHA * x_glu) * (x_lin + 1.0))
        act = jnp.concatenate(acts, axis=1)
        y = jnp.dot(act.astype(BF16), wd[...], preferred_element_type=F32) + dnb_ref[...]
        _store_row_tiles(y_ref, y)

    @pl.when((i >= n_live) & (i <= last))
    def _():
        ring.wait_rows(i)

    @pl.when(i >= n_live)
    def _():
        y_ref[...] = jnp.zeros(y_ref.shape, F32)


def _experts(blk_e, nused, row_tok, xn2, gu_w, gu_b, dn_w, dn_b, perm):
    nb, _, bm = row_tok.shape
    f2 = gu_w.shape[2]
    f, d = dn_w.shape[1:]
    c = d // LANES
    wspec = lambda shape: pl.BlockSpec((None,) + shape, lambda i, be, nu: (be[i], 0, 0))
    grid_spec = pltpu.PrefetchScalarGridSpec(
        num_scalar_prefetch=2,
        grid=(nb,),
        in_specs=[pl.BlockSpec(memory_space=pl.ANY), pl.BlockSpec(memory_space=pl.ANY),
                  wspec((d, f2)), wspec((1, f2)), wspec((f, d)), wspec((1, d)),
                  pl.BlockSpec(perm.shape, lambda i, be, nu: (0, 0))],
        out_specs=pl.BlockSpec((bm * c, LANES), lambda i, be, nu: (i, 0)),
        scratch_shapes=[pltpu.VMEM((EXPERT_RING, bm * c, LANES), F32), pltpu.VMEM((d, f2), BF16),
                        pltpu.VMEM((f, d), BF16), pltpu.SMEM((EXPERT_RING, 1, bm), I32),
                        pltpu.SemaphoreType.DMA((EXPERT_RING,)),
                        pltpu.SemaphoreType.DMA((EXPERT_RING,))],
    )
    return pl.pallas_call(
        _expert_body,
        grid_spec=grid_spec,
        out_shape=jax.ShapeDtypeStruct((nb * bm * c, LANES), F32),
        compiler_params=_params(("arbitrary",)),
        name="experts",
    )(blk_e, nused, row_tok, xn2, gu_w, gu_b, dn_w, dn_b, perm)


def _combine_body(dest_hbm, yb_hbm, h_ref, gate_ref, nw_ref, o_ref, ybuf, dest_smem, isem, gsem):
    i = pl.program_id(0)
    nt = pl.num_programs(0)
    tm = h_ref.shape[0]
    ring = _RowRing(dest_hbm, yb_hbm, dest_smem, ybuf, isem, gsem)

    @pl.when(i == 0)
    def _():
        ring.prologue()

    ring.start_next_idx(i, nt - 1)

    @pl.when(i + ring.ahead <= nt - 1)
    def _():
        ring.request(i)

    ring.wait_rows(i)
    rows = ybuf.at[i % ring.depth]
    c = ring.c
    gates = gate_ref[...]
    acc = h_ref[...]
    for kk in range(TOP_K):
        acc = acc + gates[:, kk:kk + 1] * _load_row_tiles(rows, tm, c, first=kk * tm * c)
    o_ref[...] = _rms(acc, nw_ref[...])


def _combine(dest_tiles, yb, h, gates_t, nfw):
    t, d = h.shape
    nt, _, n = dest_tiles.shape
    tm = n // TOP_K
    row = lambda i: (i, 0)
    return pl.pallas_call(
        _combine_body,
        grid=(nt,),
        in_specs=[pl.BlockSpec(memory_space=pl.ANY), pl.BlockSpec(memory_space=pl.ANY),
                  pl.BlockSpec((tm, d), row), pl.BlockSpec((tm, TOP_K), row),
                  pl.BlockSpec((1, d), lambda i: (0, 0))],
        out_specs=pl.BlockSpec((tm, d), row),
        out_shape=jax.ShapeDtypeStruct((t, d), F32),
        scratch_shapes=[pltpu.VMEM((2, n * (d // LANES), LANES), F32), pltpu.SMEM((2, 1, n), I32),
                        pltpu.SemaphoreType.DMA((2,)), pltpu.SemaphoreType.DMA((2,))],
        compiler_params=_params(("arbitrary",)),
        name="combine",
    )(dest_tiles, yb, h, gates_t, nfw)


def _head_ones():
    hi = lax.broadcasted_iota(I32, (RWKV_WIDTH, RWKV_WIDTH), 0) // HEAD
    hj = lax.broadcasted_iota(I32, (RWKV_WIDTH, RWKV_WIDTH), 1) // HEAD
    return (hi == hj).astype(BF16)


def _layer(x2, bsz, seq, norm_mix_w, w_in, conv_w, mu_shift, w0, w_decay_up, a0, w_iclr_up,
           w_gate_up, k_k, k_a, r_k, ln_x_w, ln_x_b, w_out, norm_ffn_w, router_w, router_b,
           gu_w, gu_b, dn_w, dn_b):
    t, d = x2.shape
    row1 = lambda a: a.reshape(1, -1)

    p = _inproj(x2, row1(norm_mix_w), w_in.astype(BF16))
    yconv, r, k, v, lw, a, b, g, bonus = _prep(
        p, bsz, seq, conv_w, row1(mu_shift), row1(w0), w_decay_up, row1(a0), w_iclr_up,
        w_gate_up, row1(k_k), row1(k_a), row1(r_k), _head_ones())
    yn = _scan(r, k, v, lw, a, b, bsz, seq)
    h, xn2, idx, gates, rank, cnt = _mixout(
        yn, bonus, g, yconv, x2, row1(ln_x_w), row1(ln_x_b), w_out.astype(BF16),
        row1(norm_ffn_w), router_w.T, router_b.reshape(-1, 1))

    bm = ROW_BLOCK
    counts = cnt[:, 0].astype(I32)
    pcounts = (counts + bm - 1) // bm * bm
    pends = jnp.cumsum(pcounts)
    pstarts = pends - pcounts
    eids = jnp.arange(N_EXPERTS, dtype=I32)
    dest = rank + jnp.sum(jnp.where(idx[None] == eids[:, None, None], pstarts[:, None, None], 0),
                          axis=0)
    n_blocks = t * TOP_K // bm + N_EXPERTS + EXPERT_RING - 2
    n_rows = n_blocks * bm
    tok_of_pair = jnp.tile(jnp.arange(t, dtype=I32), TOP_K)
    row_tok = jnp.zeros((n_rows,), I32).at[dest.reshape(-1)].set(tok_of_pair, unique_indices=True)
    blk_start = jnp.arange(n_blocks, dtype=I32) * bm
    blk_e = jnp.minimum(jnp.sum((pends[None, :] <= blk_start[:, None]).astype(I32), axis=1),
                        N_EXPERTS - 1)
    nused = (pends[-1] // bm).astype(I32).reshape(1)

    grp = 2 * LANES
    f2 = gu_w.shape[2]
    src = jnp.arange(grp, dtype=I32)[:, None]
    dst = jnp.arange(grp, dtype=I32)[None, :]
    perm = (src == jnp.where(dst < LANES, 2 * dst, 2 * (dst - LANES) + 1)).astype(BF16)
    gub = gu_b.reshape(N_EXPERTS, f2 // grp, LANES, 2).transpose(0, 1, 3, 2).reshape(N_EXPERTS, 1, f2)
    yb = _experts(blk_e, nused, row_tok.reshape(n_blocks, 1, bm), xn2, gu_w, gub, dn_w,
                  dn_b.reshape(N_EXPERTS, 1, d), perm)

    tm = min(t, 256)
    nt = t // tm
    assert nt >= 2
    dest_tiles = dest.reshape(TOP_K, nt, tm).transpose(1, 0, 2).reshape(nt, 1, TOP_K * tm)
    return dest_tiles, yb, h, gates.T


def kernel(x, norm_mix_w, w_in, conv_w, mu_shift, w0, w_decay_up, a0, w_iclr_up, w_gate_up,
           k_k, k_a, r_k, ln_x_w, ln_x_b, w_out, norm_ffn_w, router_w, router_b,
           gu_w, gu_b, dn_w, dn_b, norm_final_w):
    bsz, seq, d = x.shape
    assert w_in.shape[0] == 1, "the final RMSNorm is fused into the single layer's combine step"
    dest_tiles, yb, h_mid, gates_t = _layer(
        x.reshape(bsz * seq, d), bsz, seq, norm_mix_w[0], w_in[0], conv_w[0], mu_shift[0], w0[0],
        w_decay_up[0], a0[0], w_iclr_up[0], w_gate_up[0], k_k[0], k_a[0], r_k[0], ln_x_w[0],
        ln_x_b[0], w_out[0], norm_ffn_w[0], router_w[0], router_b[0], gu_w[0], gu_b[0], dn_w[0],
        dn_b[0])
    out = _combine(dest_tiles, yb, h_mid, gates_t, norm_final_w.reshape(1, d))
    return out.reshape(bsz, seq, d)
```

```python
import functools

import jax
import jax.numpy as jnp
from jax import lax
from jax.experimental import pallas as pl
from jax.experimental.pallas import tpu as pltpu

F32 = jnp.float32
BF16 = jnp.bfloat16
I32 = jnp.int32

CONV_WIDTH = 512
CONV_K = 3
RWKV_WIDTH = 512
HEAD = 64
HEADS = RWKV_WIDTH // HEAD
LORA_DECAY = 64
LORA_ICLR = 64
LORA_GATE = 128
N_EXPERTS = 32
TOP_K = 4
SWIGLU_LIMIT = 7.0
SWIGLU_ALPHA = 1.702
RMS_EPS = 1e-5
GN_EPS = 64e-5

CHUNK = 64
SUB = 16
SCAN_CHUNKS = 2
ROW_BLOCK = 512
EXPERT_RING = 2
LANES = 128
VMEM_LIMIT = 56 * 1024 * 1024


def _params(sem, vmem=VMEM_LIMIT):
    return pltpu.CompilerParams(dimension_semantics=sem, vmem_limit_bytes=vmem)


def _mm(a, b):
    return jnp.dot(a.astype(BF16), b.astype(BF16), preferred_element_type=F32)


def _mm_nt(a, b):
    return lax.dot_general(a.astype(BF16), b.astype(BF16), (((1,), (1,)), ((), ())),
                           preferred_element_type=F32)


def _mm_tn(a, b):
    return lax.dot_general(a.astype(BF16), b.astype(BF16), (((0,), (0,)), ((), ())),
                           preferred_element_type=F32)


def _split_dot(x, m01, terms, left):
    acc = None
    rem = x
    for _ in range(terms):
        piece = rem.astype(BF16)
        rem = rem - piece.astype(F32)
        part = (jnp.dot(m01, piece, preferred_element_type=F32) if left
                else jnp.dot(piece, m01, preferred_element_type=F32))
        acc = part if acc is None else acc + part
    return acc


def _sigmoid(x):
    return 1.0 / (1.0 + jnp.exp(-x))


def _rms(x, w):
    return x * lax.rsqrt(jnp.mean(x * x, axis=-1, keepdims=True) + RMS_EPS) * w


def _store_row_tiles(ref, x):
    n, width = x.shape
    c = width // LANES
    for j in range(c):
        ref[pl.ds(j, n, stride=c), :] = x[:, j * LANES:(j + 1) * LANES]


def _load_row_tiles(ref, n, c, first=0):
    return jnp.concatenate([ref[pl.ds(first + j, n, stride=c), :] for j in range(c)], axis=1)


def _inproj_body(x_ref, nw_ref, w_ref, p_ref):
    xn = _rms(x_ref[...], nw_ref[...])
    p_ref[...] = jnp.dot(xn.astype(BF16), w_ref[...], preferred_element_type=F32)


def _inproj(x2, nw, w_bf):
    t, d = x2.shape
    c = w_bf.shape[1]
    tm = min(t, 512)
    return pl.pallas_call(
        _inproj_body,
        grid=(t // tm,),
        in_specs=[pl.BlockSpec((tm, d), lambda i: (i, 0)),
                  pl.BlockSpec((1, d), lambda i: (0, 0)),
                  pl.BlockSpec((d, c), lambda i: (0, 0))],
        out_specs=pl.BlockSpec((tm, c), lambda i: (i, 0)),
        out_shape=jax.ShapeDtypeStruct((t, c), F32),
        compiler_params=_params(("parallel",)),
        name="inproj",
    )(x2, nw, w_bf)


def _prep_body(p_ref, convw_ref, mu_ref, w0_ref, wdu_ref, a0_ref, wiu_ref, wgu_ref,
               kk_ref, ka_ref, rk_ref, ones_ref,
               yconv_ref, r_ref, k_ref, v_ref, lw_ref, a_ref, b_ref, g_ref, bonus_ref,
               buf_ref):
    ts = p_ref.shape[0]
    cw, rw = CONV_WIDTH, RWKV_WIDTH
    o_r = 3 * cw

    @pl.when(pl.program_id(1) == 0)
    def _():
        buf_ref[0:8, :] = jnp.zeros((8, buf_ref.shape[1]), F32)

    buf_ref[8:8 + ts, :] = p_ref[...]

    u0 = buf_ref[8:8 + ts, cw:2 * cw] * buf_ref[8:8 + ts, 2 * cw:3 * cw]
    u1 = buf_ref[7:7 + ts, cw:2 * cw] * buf_ref[7:7 + ts, 2 * cw:3 * cw]
    u2 = buf_ref[6:6 + ts, cw:2 * cw] * buf_ref[6:6 + ts, 2 * cw:3 * cw]
    conv = convw_ref[0:1, :] * u2 + convw_ref[1:2, :] * u1 + convw_ref[2:3, :] * u0
    yconv_ref[...] = buf_ref[8:8 + ts, 0:cw] * conv

    def shifted(lo, hi):
        cur = buf_ref[8:8 + ts, o_r + lo:o_r + hi]
        prev = buf_ref[7:7 + ts, o_r + lo:o_r + hi]
        return cur + (prev - cur) * mu_ref[:, lo:hi]

    r = shifted(0, rw)
    k = shifted(rw, 2 * rw)
    v = shifted(2 * rw, 3 * rw)
    o1 = 3 * rw
    d_w = shifted(o1, o1 + LORA_DECAY)
    d_a = shifted(o1 + LORA_DECAY, o1 + LORA_DECAY + LORA_ICLR)
    d_g = shifted(o1 + LORA_DECAY + LORA_ICLR, o1 + LORA_DECAY + LORA_ICLR + LORA_GATE)

    z = -(w0_ref[...] + _mm(jnp.tanh(d_w), wdu_ref[...]))
    softplus = jnp.maximum(z, 0.0) + jnp.log(1.0 + jnp.exp(-jnp.abs(z)))
    w_log = -softplus - 0.5
    lw_ref[...] = -jnp.exp(w_log)
    iclr = _sigmoid(a0_ref[...] + _mm(d_a, wiu_ref[...]))
    g_ref[...] = _mm(_sigmoid(d_g), wgu_ref[...])

    ones = ones_ref[...]
    kk = k * kk_ref[...]
    nrm = jnp.sqrt(_split_dot(kk * kk, ones, 2, left=False))
    kk = kk / jnp.maximum(nrm, 1e-12)
    k2 = k * (1.0 + (iclr - 1.0) * ka_ref[...])
    r_ref[...] = r
    k_ref[...] = k2
    v_ref[...] = v
    a_ref[...] = -kk
    b_ref[...] = kk * iclr
    bonus_ref[...] = _split_dot(r * k2 * rk_ref[...], ones, 2, left=False) * v

    buf_ref[0:8, :] = buf_ref[ts:ts + 8, :]


def _prep(p, bsz, seq, convw, mu, w0, wdu, a0, wiu, wgu, kkw, kaw, rkw, ones):
    t, c = p.shape
    ts = min(seq, 256)
    ns = seq // ts
    row = lambda b, s: (b * ns + s, 0)
    const = lambda b, s: (0, 0)
    full = lambda arr: pl.BlockSpec(arr.shape, const)
    out_spec = pl.BlockSpec((ts, RWKV_WIDTH), row)
    out_sds = jax.ShapeDtypeStruct((t, RWKV_WIDTH), F32)
    return pl.pallas_call(
        _prep_body,
        grid=(bsz, ns),
        in_specs=[pl.BlockSpec((ts, c), row)] + [full(a) for a in
                  (convw, mu, w0, wdu, a0, wiu, wgu, kkw, kaw, rkw, ones)],
        out_specs=[out_spec] * 9,
        out_shape=[out_sds] * 9,
        scratch_shapes=[pltpu.VMEM((ts + 8, c), F32)],
        compiler_params=_params(("parallel", "arbitrary")),
        name="prep",
    )(p, convw, mu, w0, wdu, a0, wiu, wgu, kkw, kaw, rkw, ones)


def _unit_lower_inverse(p_low, eye, bd_mask):
    pd = [jnp.where(bd_mask, p, 0.0) for p in p_low]
    po = [p - x for p, x in zip(p_low, pd)]
    d = [eye + x for x in pd]
    pk = pd
    for _ in range(3):
        pk = [_mm(x, x) for x in pk]
        dd = [_mm(x, y) for x, y in zip(d, pk)]
        d = [x + y for x, y in zip(d, dd)]
    q = [_mm(x, y) for x, y in zip(d, po)]
    q2 = [_mm(x, x) for x in q]
    q3 = [_mm(x, y) for x, y in zip(q, q2)]
    w = [eye + a + b + c for a, b, c in zip(q, q2, q3)]
    return [_mm(x, y) for x, y in zip(w, d)]


def _scan_body(r_ref, k_ref, v_ref, lw_ref, a_ref, b_ref, y_ref, s_ref):
    L = CHUNK

    @pl.when(pl.program_id(1) == 0)
    def _():
        s_ref[...] = jnp.zeros(s_ref.shape, F32)

    row = lax.broadcasted_iota(I32, (L, L), 0)
    col = lax.broadcasted_iota(I32, (L, L), 1)
    lower = row >= col
    strict = row > col
    eye = (row == col).astype(F32)
    bd_mask = (row // SUB) == (col // SUB)

    rows = lw_ref.shape[0]
    nsub = rows // L
    rr = lax.broadcasted_iota(I32, (rows, rows), 0)
    cc = lax.broadcasted_iota(I32, (rows, rows), 1)
    chunk_lower = ((rr >= cc) & (rr // L == cc // L)).astype(BF16)
    lw = lw_ref[...]
    c = _split_dot(lw, chunk_lower, 3, left=True)
    e_in = jnp.exp(c)
    e_inv = jnp.exp(-c)
    e_ex = jnp.exp(c - lw)
    a_t = a_ref[...] * e_ex
    r_t = r_ref[...] * e_in
    b_t = b_ref[...] * e_inv
    k_t = k_ref[...] * e_inv
    v = v_ref[...]

    hs = range(HEADS)
    sls = [slice(h * HEAD, (h + 1) * HEAD) for h in hs]
    items = [(slice(s * L, (s + 1) * L), sl) for s in range(nsub) for sl in sls]
    g_last = [e_in[(s + 1) * L - 1:(s + 1) * L, :] for s in range(nsub)]
    ar = [jnp.concatenate([a_t[ts, sl], r_t[ts, sl]], axis=0) for ts, sl in items]
    gb = [_mm_nt(x, b_t[ts, sl]) for x, (ts, sl) in zip(ar, items)]
    gk = [_mm_nt(x, k_t[ts, sl]) for x, (ts, sl) in zip(ar, items)]
    p_low = [jnp.where(strict, x[0:L], 0.0) for x in gb]
    m_br = [jnp.where(lower, x[L:2 * L], 0.0) for x in gb]
    m_akr = [jnp.concatenate([jnp.where(strict, x[0:L], 0.0), jnp.where(lower, x[L:2 * L], 0.0)],
                             axis=0) for x in gk]
    vh = [v[ts, sl] for ts, sl in items]
    mv = [_mm(x, y) for x, y in zip(m_akr, vh)]
    t_inv = _unit_lower_inverse(p_low, eye, bd_mask)

    state = [s_ref[h] for h in hs]
    for s in range(nsub):
        it = [s * HEADS + h for h in hs]
        ts = items[it[0]][0]
        ars = [_mm_nt(ar[j], state[h]) for h, j in zip(hs, it)]
        u = [_mm(t_inv[j], ars[h][0:L] + mv[j][0:L]) for h, j in zip(hs, it)]
        mu = [_mm(m_br[j], u[h]) for h, j in zip(hs, it)]
        s_add = [_mm_tn(jnp.concatenate([u[h], vh[j]], axis=0),
                        jnp.concatenate([b_t[ts, sls[h]], k_t[ts, sls[h]]], axis=0)
                        * g_last[s][:, sls[h]]) for h, j in zip(hs, it)]
        for h, j in zip(hs, it):
            state[h] = state[h] * g_last[s][:, sls[h]] + s_add[h]
            y = ars[h][L:2 * L] + mu[h] + mv[j][L:2 * L]
            mean = jnp.mean(y, axis=-1, keepdims=True)
            yc = y - mean
            var = jnp.mean(yc * yc, axis=-1, keepdims=True)
            y_ref[ts, sls[h]] = yc * lax.rsqrt(var + GN_EPS)
    for h in hs:
        s_ref[h] = state[h]


def _scan(r, k, v, lw, a, b, bsz, seq):
    t = r.shape[0]
    rows = CHUNK * SCAN_CHUNKS
    nc = seq // rows
    spec = pl.BlockSpec((rows, RWKV_WIDTH), lambda bi, ci: (bi * nc + ci, 0))
    return pl.pallas_call(
        _scan_body,
        grid=(bsz, nc),
        in_specs=[spec] * 6,
        out_specs=spec,
        out_shape=jax.ShapeDtypeStruct((t, RWKV_WIDTH), F32),
        scratch_shapes=[pltpu.VMEM((HEADS, HEAD, HEAD), F32)],
        compiler_params=_params(("parallel", "arbitrary")),
        name="scan",
    )(r, k, v, lw, a, b)


def _mixout_body(yn_ref, bonus_ref, g_ref, yconv_ref, x_ref, lnw_ref, lnb_ref, wout_ref,
                 nfw_ref, rwt_ref, rb_ref,
                 h_ref, xn_ref, idx_ref, gate_ref, rank_ref, cnt_ref, carry_ref):
    tm = x_ref.shape[0]

    @pl.when(pl.program_id(0) == 0)
    def _():
        carry_ref[...] = jnp.zeros(carry_ref.shape, F32)

    y_rwkv = (yn_ref[...] * lnw_ref[...] + lnb_ref[...] + bonus_ref[...]) * g_ref[...]
    cat = jnp.concatenate([yconv_ref[...], y_rwkv], axis=1)
    h = x_ref[...] + jnp.dot(cat.astype(BF16), wout_ref[...], preferred_element_type=F32)
    h_ref[...] = h
    xn = _rms(h, nfw_ref[...])
    _store_row_tiles(xn_ref, xn)

    lt = lax.dot_general(rwt_ref[...], xn, (((1,), (1,)), ((), ())),
                         precision=lax.Precision.HIGHEST, preferred_element_type=F32)
    lt = lt + rb_ref[...]
    eio = lax.broadcasted_iota(I32, lt.shape, 0)
    vals, onehots = [], []
    cur = lt
    for kk in range(TOP_K):
        m = jnp.max(cur, axis=0, keepdims=True)
        ix = jnp.min(jnp.where(cur == m, eio, N_EXPERTS), axis=0, keepdims=True)
        sel = eio == ix
        idx_ref[kk:kk + 1, :] = ix
        vals.append(m)
        onehots.append(sel.astype(F32))
        cur = jnp.where(sel, -jnp.inf, cur)
    exps = [jnp.exp(vv - vals[0]) for vv in vals]
    den = exps[0] + exps[1] + exps[2] + exps[3]
    for kk in range(TOP_K):
        gate_ref[kk:kk + 1, :] = exps[kk] / den

    trow = lax.broadcasted_iota(I32, (tm, tm), 0)
    tcol = lax.broadcasted_iota(I32, (tm, tm), 1)
    tri = (trow <= tcol).astype(BF16)
    ohs = jnp.concatenate(onehots, axis=0).astype(BF16)
    pref = jnp.dot(ohs, tri, preferred_element_type=F32)
    base = carry_ref[...]
    for kk in range(TOP_K):
        pk = pref[kk * N_EXPERTS:(kk + 1) * N_EXPERTS]
        rk = jnp.sum(onehots[kk] * (pk - 1.0 + base), axis=0, keepdims=True)
        rank_ref[kk:kk + 1, :] = rk.astype(I32)
        base = base + jnp.sum(onehots[kk], axis=1, keepdims=True)
    carry_ref[...] = base
    cnt_ref[...] = base


def _mixout(yn, bonus, g, yconv, x2, lnw, lnb, wout_bf, nfw, rwt, rb):
    t, d = x2.shape
    tm = min(t, 512)
    row = lambda i: (i, 0)
    const = lambda i: (0, 0)
    colb = lambda i: (0, i)
    half = pl.BlockSpec((tm, RWKV_WIDTH), row)
    full = lambda arr: pl.BlockSpec(arr.shape, const)
    return pl.pallas_call(
        _mixout_body,
        grid=(t // tm,),
        in_specs=[half, half, half, half, pl.BlockSpec((tm, d), row),
                  full(lnw), full(lnb), full(wout_bf), full(nfw), full(rwt), full(rb)],
        out_specs=[pl.BlockSpec((tm, d), row), pl.BlockSpec((tm * (d // LANES), LANES), row),
                   pl.BlockSpec((TOP_K, tm), colb), pl.BlockSpec((TOP_K, tm), colb),
                   pl.BlockSpec((TOP_K, tm), colb), pl.BlockSpec((N_EXPERTS, 1), const)],
        out_shape=[jax.ShapeDtypeStruct((t, d), F32),
                   jax.ShapeDtypeStruct((t * (d // LANES), LANES), F32),
                   jax.ShapeDtypeStruct((TOP_K, t), I32), jax.ShapeDtypeStruct((TOP_K, t), F32),
                   jax.ShapeDtypeStruct((TOP_K, t), I32),
                   jax.ShapeDtypeStruct((N_EXPERTS, 1), F32)],
        scratch_shapes=[pltpu.VMEM((N_EXPERTS, 1), F32)],
        compiler_params=_params(("arbitrary",)),
        name="mix_out",
    )(yn, bonus, g, yconv, x2, lnw, lnb, wout_bf, nfw, rwt, rb)


class _RowRing:
    def __init__(self, idx_hbm, src_hbm, idx_smem, rows, isem, gsem):
        self.idx_hbm, self.src_hbm, self.idx_smem, self.rows = idx_hbm, src_hbm, idx_smem, rows
        self.isem, self.gsem = isem, gsem
        self.depth, _, self.n = idx_smem.shape
        self.ahead = self.depth - 1
        self.c = rows.shape[1] // self.n

    def idx_copy(self, blk):
        slot = blk % self.depth
        return pltpu.make_async_copy(self.idx_hbm.at[blk], self.idx_smem.at[slot],
                                     self.isem.at[slot])

    def row_copy(self, slot, src_row, r):
        c = self.c
        first = src_row * c if isinstance(src_row, int) else pl.multiple_of(src_row * c, c)
        return pltpu.make_async_copy(self.src_hbm.at[pl.ds(first, c)],
                                     self.rows.at[slot, pl.ds(r * c, c)], self.gsem.at[slot])

    def issue_rows(self, blk, unrolled):
        slot = blk % self.depth
        if unrolled:
            for r in range(self.n):
                self.row_copy(slot, self.idx_smem[slot, 0, r], r).start(priority=r % 2)
        else:
            def body(r, carry):
                self.row_copy(slot, self.idx_smem[slot, 0, r], r).start()
                return carry
            lax.fori_loop(0, self.n, body, 0, unroll=8)

    def wait_rows(self, blk):
        slot = blk % self.depth
        for r in range(self.n):
            self.row_copy(slot, 0, r).wait()

    def prologue(self):
        for b in range(self.ahead):
            first = self.idx_copy(b)
            first.start()
            first.wait()
            self.issue_rows(b, unrolled=False)
        self.idx_copy(self.ahead).start()

    def start_next_idx(self, i, last):
        @pl.when(i + self.depth <= last)
        def _():
            self.idx_copy(i + self.depth).start()

    def request(self, i):
        self.idx_copy(i + self.ahead).wait()
        self.issue_rows(i + self.ahead, unrolled=True)


def _invert_body(dest_hbm, o_ref, dsm, sem):
    i = pl.program_id(0)
    n = dsm.shape[2]
    tm = n // TOP_K

    def tile_copy(b):
        return pltpu.make_async_copy(dest_hbm.at[b], dsm.at[b % 2], sem.at[b % 2])

    @pl.when(i == 0)
    def _():
        tile_copy(0).start()

        def zero(q, carry):
            o_ref[q] = 0
            return carry

        lax.fori_loop(0, o_ref.shape[0], zero, 0, unroll=16)

    @pl.when(i + 1 < pl.num_programs(0))
    def _():
        tile_copy(i + 1).start()

    tile_copy(i).wait()
    slot = i % 2
    first_tok = i * tm
    for r in range(n):
        o_ref[dsm[slot, 0, r]] = first_tok + (r % tm)


def _invert(dest_tiles, n_rows):
    nt, _, n = dest_tiles.shape
    return pl.pallas_call(
        _invert_body,
        grid=(nt,),
        in_specs=[pl.BlockSpec(memory_space=pl.ANY)],
        out_specs=pl.BlockSpec(memory_space=pltpu.SMEM),
        out_shape=jax.ShapeDtypeStruct((n_rows,), I32),
        scratch_shapes=[pltpu.SMEM((2, 1, n), I32), pltpu.SemaphoreType.DMA((2,))],
        compiler_params=_params(("arbitrary",)),
        name="invert",
    )(dest_tiles)


def _expert_body(blk_e_ref, nused_ref, tok_hbm, xn_hbm, gu_ref, gub_ref, dn_ref, dnb_ref, perm_ref,
                 y_ref, xbuf, wgl, wd, tok_smem, isem, gsem):
    i = pl.program_id(0)
    n_live = nused_ref[0]
    ring = _RowRing(tok_hbm, xn_hbm, tok_smem, xbuf, isem, gsem)
    grp = perm_ref.shape[0]
    half = grp // 2

    @pl.when(i == 0)
    def _():
        ring.prologue()

    last = n_live + ring.ahead - 1
    ring.start_next_idx(i, last)

    new_expert = (i == 0) | (blk_e_ref[i] != blk_e_ref[jnp.maximum(i - 1, 0)])

    @pl.when((i < n_live) & new_expert)
    def _():
        for c in range(gu_ref.shape[1] // grp):
            cols = slice(c * grp, (c + 1) * grp)
            wgl[:, cols] = jnp.dot(gu_ref[:, cols].astype(BF16), perm_ref[...],
                                   preferred_element_type=F32).astype(BF16)
        wd[...] = dn_ref[...].astype(BF16)

    @pl.when(i < n_live)
    def _():
        ring.request(i)
        ring.wait_rows(i)
        xb = _load_row_tiles(xbuf.at[i % ring.depth], ring.n, ring.c).astype(BF16)
        h = jnp.dot(xb, wgl[...], preferred_element_type=F32) + gub_ref[...]
        acts = []
        for c in range(h.shape[1] // grp):
            x_glu = jnp.minimum(h[:, c * grp:c * grp + half], SWIGLU_LIMIT)
            x_lin = jnp.clip(h[:, c * grp + half:(c + 1) * grp], -SWIGLU_LIMIT, SWIGLU_LIMIT)
            acts.append(x_glu * _sigmoid(SWIGLU_ALPHA * x_glu) * (x_lin + 1.0))
        act = jnp.concatenate(acts, axis=1)
        y = jnp.dot(act.astype(BF16), wd[...], preferred_element_type=F32) + dnb_ref[...]
        _store_row_tiles(y_ref, y)

    @pl.when((i >= n_live) & (i <= last))
    def _():
        ring.wait_rows(i)

    @pl.when(i >= n_live)
    def _():
        y_ref[...] = jnp.zeros(y_ref.shape, F32)


def _experts(blk_e, nused, row_tok, xn2, gu_w, gu_b, dn_w, dn_b, perm):
    nb, _, bm = row_tok.shape
    f2 = gu_w.shape[2]
    f, d = dn_w.shape[1:]
    c = d // LANES
    wspec = lambda shape: pl.BlockSpec((None,) + shape, lambda i, be, nu: (be[i], 0, 0))
    grid_spec = pltpu.PrefetchScalarGridSpec(
        num_scalar_prefetch=2,
        grid=(nb,),
        in_specs=[pl.BlockSpec(memory_space=pl.ANY), pl.BlockSpec(memory_space=pl.ANY),
                  wspec((d, f2)), wspec((1, f2)), wspec((f, d)), wspec((1, d)),
                  pl.BlockSpec(perm.shape, lambda i, be, nu: (0, 0))],
        out_specs=pl.BlockSpec((bm * c, LANES), lambda i, be, nu: (i, 0)),
        scratch_shapes=[pltpu.VMEM((EXPERT_RING, bm * c, LANES), F32), pltpu.VMEM((d, f2), BF16),
                        pltpu.VMEM((f, d), BF16), pltpu.SMEM((EXPERT_RING, 1, bm), I32),
                        pltpu.SemaphoreType.DMA((EXPERT_RING,)),
                        pltpu.SemaphoreType.DMA((EXPERT_RING,))],
    )
    return pl.pallas_call(
        _expert_body,
        grid_spec=grid_spec,
        out_shape=jax.ShapeDtypeStruct((nb * bm * c, LANES), F32),
        compiler_params=_params(("arbitrary",)),
        name="experts",
    )(blk_e, nused, row_tok, xn2, gu_w, gu_b, dn_w, dn_b, perm)


def _combine_body(dest_hbm, yb_hbm, h_ref, gate_ref, nw_ref, o_ref, ybuf, dest_smem, isem, gsem):
    i = pl.program_id(0)
    nt = pl.num_programs(0)
    tm = h_ref.shape[0]
    ring = _RowRing(dest_hbm, yb_hbm, dest_smem, ybuf, isem, gsem)

    @pl.when(i == 0)
    def _():
        ring.prologue()

    ring.start_next_idx(i, nt - 1)

    @pl.when(i + ring.ahead <= nt - 1)
    def _():
        ring.request(i)

    ring.wait_rows(i)
    rows = ybuf.at[i % ring.depth]
    c = ring.c
    gates = gate_ref[...]
    acc = h_ref[...]
    for kk in range(TOP_K):
        acc = acc + gates[:, kk:kk + 1] * _load_row_tiles(rows, tm, c, first=kk * tm * c)
    o_ref[...] = _rms(acc, nw_ref[...])


def _combine(dest_tiles, yb, h, gates_t, nfw):
    t, d = h.shape
    nt, _, n = dest_tiles.shape
    tm = n // TOP_K
    row = lambda i: (i, 0)
    return pl.pallas_call(
        _combine_body,
        grid=(nt,),
        in_specs=[pl.BlockSpec(memory_space=pl.ANY), pl.BlockSpec(memory_space=pl.ANY),
                  pl.BlockSpec((tm, d), row), pl.BlockSpec((tm, TOP_K), row),
                  pl.BlockSpec((1, d), lambda i: (0, 0))],
        out_specs=pl.BlockSpec((tm, d), row),
        out_shape=jax.ShapeDtypeStruct((t, d), F32),
        scratch_shapes=[pltpu.VMEM((2, n * (d // LANES), LANES), F32), pltpu.SMEM((2, 1, n), I32),
                        pltpu.SemaphoreType.DMA((2,)), pltpu.SemaphoreType.DMA((2,))],
        compiler_params=_params(("arbitrary",)),
        name="combine",
    )(dest_tiles, yb, h, gates_t, nfw)


def _head_ones():
    hi = lax.broadcasted_iota(I32, (RWKV_WIDTH, RWKV_WIDTH), 0) // HEAD
    hj = lax.broadcasted_iota(I32, (RWKV_WIDTH, RWKV_WIDTH), 1) // HEAD
    return (hi == hj).astype(BF16)


def _layer(x2, bsz, seq, norm_mix_w, w_in, conv_w, mu_shift, w0, w_decay_up, a0, w_iclr_up,
           w_gate_up, k_k, k_a, r_k, ln_x_w, ln_x_b, w_out, norm_ffn_w, router_w, router_b,
           gu_w, gu_b, dn_w, dn_b):
    t, d = x2.shape
    row1 = lambda a: a.reshape(1, -1)

    p = _inproj(x2, row1(norm_mix_w), w_in.astype(BF16))
    yconv, r, k, v, lw, a, b, g, bonus = _prep(
        p, bsz, seq, conv_w, row1(mu_shift), row1(w0), w_decay_up, row1(a0), w_iclr_up,
        w_gate_up, row1(k_k), row1(k_a), row1(r_k), _head_ones())
    yn = _scan(r, k, v, lw, a, b, bsz, seq)
    h, xn2, idx, gates, rank, cnt = _mixout(
        yn, bonus, g, yconv, x2, row1(ln_x_w), row1(ln_x_b), w_out.astype(BF16),
        row1(norm_ffn_w), router_w.T, router_b.reshape(-1, 1))

    bm = ROW_BLOCK
    counts = cnt[:, 0].astype(I32)
    pcounts = (counts + bm - 1) // bm * bm
    pends = jnp.cumsum(pcounts)
    pstarts = pends - pcounts
    eids = jnp.arange(N_EXPERTS, dtype=I32)
    dest = rank + jnp.sum(jnp.where(idx[None] == eids[:, None, None], pstarts[:, None, None], 0),
                          axis=0)
    n_blocks = t * TOP_K // bm + N_EXPERTS + EXPERT_RING - 2
    n_rows = n_blocks * bm
    tm = min(t, 256)
    nt = t // tm
    assert nt >= 2
    dest_tiles = dest.reshape(TOP_K, nt, tm).transpose(1, 0, 2).reshape(nt, 1, TOP_K * tm)
    row_tok = _invert(dest_tiles, n_rows)
    blk_start = jnp.arange(n_blocks, dtype=I32) * bm
    blk_e = jnp.minimum(jnp.sum((pends[None, :] <= blk_start[:, None]).astype(I32), axis=1),
                        N_EXPERTS - 1)
    nused = (pends[-1] // bm).astype(I32).reshape(1)

    grp = 2 * LANES
    f2 = gu_w.shape[2]
    src = jnp.arange(grp, dtype=I32)[:, None]
    dst = jnp.arange(grp, dtype=I32)[None, :]
    perm = (src == jnp.where(dst < LANES, 2 * dst, 2 * (dst - LANES) + 1)).astype(BF16)
    gub = gu_b.reshape(N_EXPERTS, f2 // grp, LANES, 2).transpose(0, 1, 3, 2).reshape(N_EXPERTS, 1, f2)
    yb = _experts(blk_e, nused, row_tok.reshape(n_blocks, 1, bm), xn2, gu_w, gub, dn_w,
                  dn_b.reshape(N_EXPERTS, 1, d), perm)

    return dest_tiles, yb, h, gates.T


def kernel(x, norm_mix_w, w_in, conv_w, mu_shift, w0, w_decay_up, a0, w_iclr_up, w_gate_up,
           k_k, k_a, r_k, ln_x_w, ln_x_b, w_out, norm_ffn_w, router_w, router_b,
           gu_w, gu_b, dn_w, dn_b, norm_final_w):
    bsz, seq, d = x.shape
    assert w_in.shape[0] == 1, "the final RMSNorm is fused into the single layer's combine step"
    dest_tiles, yb, h_mid, gates_t = _layer(
        x.reshape(bsz * seq, d), bsz, seq, norm_mix_w[0], w_in[0], conv_w[0], mu_shift[0], w0[0],
        w_decay_up[0], a0[0], w_iclr_up[0], w_gate_up[0], k_k[0], k_a[0], r_k[0], ln_x_w[0],
        ln_x_b[0], w_out[0], norm_ffn_w[0], router_w[0], router_b[0], gu_w[0], gu_b[0], dn_w[0],
        dn_b[0])
    out = _combine(dest_tiles, yb, h_mid, gates_t, norm_final_w.reshape(1, d))
    return out.reshape(bsz, seq, d)
```

```python
import functools

import jax
import jax.numpy as jnp
from jax import lax
from jax.experimental import pallas as pl
from jax.experimental.pallas import tpu as pltpu

F32 = jnp.float32
BF16 = jnp.bfloat16
I32 = jnp.int32

CONV_WIDTH = 512
CONV_K = 3
RWKV_WIDTH = 512
HEAD = 64
HEADS = RWKV_WIDTH // HEAD
LORA_DECAY = 64
LORA_ICLR = 64
LORA_GATE = 128
N_EXPERTS = 32
TOP_K = 4
SWIGLU_LIMIT = 7.0
SWIGLU_ALPHA = 1.702
RMS_EPS = 1e-5
GN_EPS = 64e-5

CHUNK = 64
SUB = 16
SCAN_CHUNKS = 2
ROW_BLOCK = 512
LANES = 128
VMEM_LIMIT = 56 * 1024 * 1024


def _params(sem, vmem=VMEM_LIMIT):
    return pltpu.CompilerParams(dimension_semantics=sem, vmem_limit_bytes=vmem)


def _mm(a, b):
    return jnp.dot(a.astype(BF16), b.astype(BF16), preferred_element_type=F32)


def _mm_nt(a, b):
    return lax.dot_general(a.astype(BF16), b.astype(BF16), (((1,), (1,)), ((), ())),
                           preferred_element_type=F32)


def _mm_tn(a, b):
    return lax.dot_general(a.astype(BF16), b.astype(BF16), (((0,), (0,)), ((), ())),
                           preferred_element_type=F32)


def _split_dot(x, m01, terms, left):
    acc = None
    rem = x
    for _ in range(terms):
        piece = rem.astype(BF16)
        rem = rem - piece.astype(F32)
        part = (jnp.dot(m01, piece, preferred_element_type=F32) if left
                else jnp.dot(piece, m01, preferred_element_type=F32))
        acc = part if acc is None else acc + part
    return acc


def _sigmoid(x):
    return 1.0 / (1.0 + jnp.exp(-x))


def _rms(x, w):
    return x * lax.rsqrt(jnp.mean(x * x, axis=-1, keepdims=True) + RMS_EPS) * w


def _store_row_tiles(ref, x):
    n, width = x.shape
    c = width // LANES
    for j in range(c):
        ref[pl.ds(j, n, stride=c), :] = x[:, j * LANES:(j + 1) * LANES]


def _load_row_tiles(ref, n, c, first=0):
    return jnp.concatenate([ref[pl.ds(first + j, n, stride=c), :] for j in range(c)], axis=1)


def _inproj_body(x_ref, nw_ref, w_ref, p_ref):
    xn = _rms(x_ref[...], nw_ref[...])
    p_ref[...] = jnp.dot(xn.astype(BF16), w_ref[...], preferred_element_type=F32)


def _inproj(x2, nw, w_bf):
    t, d = x2.shape
    c = w_bf.shape[1]
    tm = min(t, 512)
    return pl.pallas_call(
        _inproj_body,
        grid=(t // tm,),
        in_specs=[pl.BlockSpec((tm, d), lambda i: (i, 0)),
                  pl.BlockSpec((1, d), lambda i: (0, 0)),
                  pl.BlockSpec((d, c), lambda i: (0, 0))],
        out_specs=pl.BlockSpec((tm, c), lambda i: (i, 0)),
        out_shape=jax.ShapeDtypeStruct((t, c), F32),
        compiler_params=_params(("parallel",)),
        name="inproj",
    )(x2, nw, w_bf)


def _prep_body(p_ref, convw_ref, mu_ref, w0_ref, wdu_ref, a0_ref, wiu_ref, wgu_ref,
               kk_ref, ka_ref, rk_ref, ones_ref,
               yconv_ref, r_ref, k_ref, v_ref, lw_ref, a_ref, b_ref, g_ref, bonus_ref,
               buf_ref):
    ts = p_ref.shape[0]
    cw, rw = CONV_WIDTH, RWKV_WIDTH
    o_r = 3 * cw

    @pl.when(pl.program_id(1) == 0)
    def _():
        buf_ref[0:8, :] = jnp.zeros((8, buf_ref.shape[1]), F32)

    buf_ref[8:8 + ts, :] = p_ref[...]

    u0 = buf_ref[8:8 + ts, cw:2 * cw] * buf_ref[8:8 + ts, 2 * cw:3 * cw]
    u1 = buf_ref[7:7 + ts, cw:2 * cw] * buf_ref[7:7 + ts, 2 * cw:3 * cw]
    u2 = buf_ref[6:6 + ts, cw:2 * cw] * buf_ref[6:6 + ts, 2 * cw:3 * cw]
    conv = convw_ref[0:1, :] * u2 + convw_ref[1:2, :] * u1 + convw_ref[2:3, :] * u0
    yconv_ref[...] = buf_ref[8:8 + ts, 0:cw] * conv

    def shifted(lo, hi):
        cur = buf_ref[8:8 + ts, o_r + lo:o_r + hi]
        prev = buf_ref[7:7 + ts, o_r + lo:o_r + hi]
        return cur + (prev - cur) * mu_ref[:, lo:hi]

    r = shifted(0, rw)
    k = shifted(rw, 2 * rw)
    v = shifted(2 * rw, 3 * rw)
    o1 = 3 * rw
    d_w = shifted(o1, o1 + LORA_DECAY)
    d_a = shifted(o1 + LORA_DECAY, o1 + LORA_DECAY + LORA_ICLR)
    d_g = shifted(o1 + LORA_DECAY + LORA_ICLR, o1 + LORA_DECAY + LORA_ICLR + LORA_GATE)

    z = -(w0_ref[...] + _mm(jnp.tanh(d_w), wdu_ref[...]))
    softplus = jnp.maximum(z, 0.0) + jnp.log(1.0 + jnp.exp(-jnp.abs(z)))
    w_log = -softplus - 0.5
    lw_ref[...] = -jnp.exp(w_log)
    iclr = _sigmoid(a0_ref[...] + _mm(d_a, wiu_ref[...]))
    g_ref[...] = _mm(_sigmoid(d_g), wgu_ref[...])

    ones = ones_ref[...]
    kk = k * kk_ref[...]
    nrm = jnp.sqrt(_split_dot(kk * kk, ones, 2, left=False))
    kk = kk / jnp.maximum(nrm, 1e-12)
    k2 = k * (1.0 + (iclr - 1.0) * ka_ref[...])
    r_ref[...] = r
    k_ref[...] = k2
    v_ref[...] = v
    a_ref[...] = -kk
    b_ref[...] = kk * iclr
    bonus_ref[...] = _split_dot(r * k2 * rk_ref[...], ones, 2, left=False) * v

    buf_ref[0:8, :] = buf_ref[ts:ts + 8, :]


def _prep(p, bsz, seq, convw, mu, w0, wdu, a0, wiu, wgu, kkw, kaw, rkw, ones):
    t, c = p.shape
    ts = min(seq, 256)
    ns = seq // ts
    row = lambda b, s: (b * ns + s, 0)
    const = lambda b, s: (0, 0)
    full = lambda arr: pl.BlockSpec(arr.shape, const)
    out_spec = pl.BlockSpec((ts, RWKV_WIDTH), row)
    out_sds = jax.ShapeDtypeStruct((t, RWKV_WIDTH), F32)
    return pl.pallas_call(
        _prep_body,
        grid=(bsz, ns),
        in_specs=[pl.BlockSpec((ts, c), row)] + [full(a) for a in
                  (convw, mu, w0, wdu, a0, wiu, wgu, kkw, kaw, rkw, ones)],
        out_specs=[out_spec] * 9,
        out_shape=[out_sds] * 9,
        scratch_shapes=[pltpu.VMEM((ts + 8, c), F32)],
        compiler_params=_params(("parallel", "arbitrary")),
        name="prep",
    )(p, convw, mu, w0, wdu, a0, wiu, wgu, kkw, kaw, rkw, ones)


def _unit_lower_inverse(p_low, eye, bd_mask):
    pd = [jnp.where(bd_mask, p, 0.0) for p in p_low]
    po = [p - x for p, x in zip(p_low, pd)]
    d = [eye + x for x in pd]
    pk = pd
    for _ in range(3):
        pk = [_mm(x, x) for x in pk]
        dd = [_mm(x, y) for x, y in zip(d, pk)]
        d = [x + y for x, y in zip(d, dd)]
    q = [_mm(x, y) for x, y in zip(d, po)]
    q2 = [_mm(x, x) for x in q]
    q3 = [_mm(x, y) for x, y in zip(q, q2)]
    w = [eye + a + b + c for a, b, c in zip(q, q2, q3)]
    return [_mm(x, y) for x, y in zip(w, d)]


def _scan_body(r_ref, k_ref, v_ref, lw_ref, a_ref, b_ref, y_ref, s_ref):
    L = CHUNK

    @pl.when(pl.program_id(1) == 0)
    def _():
        s_ref[...] = jnp.zeros(s_ref.shape, F32)

    row = lax.broadcasted_iota(I32, (L, L), 0)
    col = lax.broadcasted_iota(I32, (L, L), 1)
    lower = row >= col
    strict = row > col
    eye = (row == col).astype(F32)
    bd_mask = (row // SUB) == (col // SUB)

    rows = lw_ref.shape[0]
    nsub = rows // L
    rr = lax.broadcasted_iota(I32, (rows, rows), 0)
    cc = lax.broadcasted_iota(I32, (rows, rows), 1)
    chunk_lower = ((rr >= cc) & (rr // L == cc // L)).astype(BF16)
    lw = lw_ref[...]
    c = _split_dot(lw, chunk_lower, 3, left=True)
    e_in = jnp.exp(c)
    e_inv = jnp.exp(-c)
    e_ex = jnp.exp(c - lw)
    a_t = a_ref[...] * e_ex
    r_t = r_ref[...] * e_in
    b_t = b_ref[...] * e_inv
    k_t = k_ref[...] * e_inv
    v = v_ref[...]

    hs = range(HEADS)
    sls = [slice(h * HEAD, (h + 1) * HEAD) for h in hs]
    items = [(slice(s * L, (s + 1) * L), sl) for s in range(nsub) for sl in sls]
    g_last = [e_in[(s + 1) * L - 1:(s + 1) * L, :] for s in range(nsub)]
    ar = [jnp.concatenate([a_t[ts, sl], r_t[ts, sl]], axis=0) for ts, sl in items]
    gb = [_mm_nt(x, b_t[ts, sl]) for x, (ts, sl) in zip(ar, items)]
    gk = [_mm_nt(x, k_t[ts, sl]) for x, (ts, sl) in zip(ar, items)]
    p_low = [jnp.where(strict, x[0:L], 0.0) for x in gb]
    m_br = [jnp.where(lower, x[L:2 * L], 0.0) for x in gb]
    m_akr = [jnp.concatenate([jnp.where(strict, x[0:L], 0.0), jnp.where(lower, x[L:2 * L], 0.0)],
                             axis=0) for x in gk]
    vh = [v[ts, sl] for ts, sl in items]
    mv = [_mm(x, y) for x, y in zip(m_akr, vh)]
    t_inv = _unit_lower_inverse(p_low, eye, bd_mask)

    state = [s_ref[h] for h in hs]
    for s in range(nsub):
        it = [s * HEADS + h for h in hs]
        ts = items[it[0]][0]
        ars = [_mm_nt(ar[j], state[h]) for h, j in zip(hs, it)]
        u = [_mm(t_inv[j], ars[h][0:L] + mv[j][0:L]) for h, j in zip(hs, it)]
        mu = [_mm(m_br[j], u[h]) for h, j in zip(hs, it)]
        s_add = [_mm_tn(jnp.concatenate([u[h], vh[j]], axis=0),
                        jnp.concatenate([b_t[ts, sls[h]], k_t[ts, sls[h]]], axis=0)
                        * g_last[s][:, sls[h]]) for h, j in zip(hs, it)]
        for h, j in zip(hs, it):
            state[h] = state[h] * g_last[s][:, sls[h]] + s_add[h]
            y = ars[h][L:2 * L] + mu[h] + mv[j][L:2 * L]
            mean = jnp.mean(y, axis=-1, keepdims=True)
            yc = y - mean
            var = jnp.mean(yc * yc, axis=-1, keepdims=True)
            y_ref[ts, sls[h]] = yc * lax.rsqrt(var + GN_EPS)
    for h in hs:
        s_ref[h] = state[h]


def _scan(r, k, v, lw, a, b, bsz, seq):
    t = r.shape[0]
    rows = CHUNK * SCAN_CHUNKS
    nc = seq // rows
    spec = pl.BlockSpec((rows, RWKV_WIDTH), lambda bi, ci: (bi * nc + ci, 0))
    return pl.pallas_call(
        _scan_body,
        grid=(bsz, nc),
        in_specs=[spec] * 6,
        out_specs=spec,
        out_shape=jax.ShapeDtypeStruct((t, RWKV_WIDTH), F32),
        scratch_shapes=[pltpu.VMEM((HEADS, HEAD, HEAD), F32)],
        compiler_params=_params(("parallel", "arbitrary")),
        name="scan",
    )(r, k, v, lw, a, b)


def _mixout_body(yn_ref, bonus_ref, g_ref, yconv_ref, x_ref, lnw_ref, lnb_ref, wout_ref,
                 nfw_ref, rwt_ref, rb_ref,
                 h_ref, xn_ref, idx_ref, gate_ref, rank_ref, cnt_ref, carry_ref):
    tm = x_ref.shape[0]

    @pl.when(pl.program_id(0) == 0)
    def _():
        carry_ref[...] = jnp.zeros(carry_ref.shape, F32)

    y_rwkv = (yn_ref[...] * lnw_ref[...] + lnb_ref[...] + bonus_ref[...]) * g_ref[...]
    cat = jnp.concatenate([yconv_ref[...], y_rwkv], axis=1)
    h = x_ref[...] + jnp.dot(cat.astype(BF16), wout_ref[...], preferred_element_type=F32)
    h_ref[...] = h
    xn = _rms(h, nfw_ref[...])
    _store_row_tiles(xn_ref, xn)

    lt = lax.dot_general(rwt_ref[...], xn, (((1,), (1,)), ((), ())),
                         precision=lax.Precision.HIGHEST, preferred_element_type=F32)
    lt = lt + rb_ref[...]
    eio = lax.broadcasted_iota(I32, lt.shape, 0)
    vals, onehots = [], []
    cur = lt
    for kk in range(TOP_K):
        m = jnp.max(cur, axis=0, keepdims=True)
        ix = jnp.min(jnp.where(cur == m, eio, N_EXPERTS), axis=0, keepdims=True)
        sel = eio == ix
        idx_ref[kk:kk + 1, :] = ix
        vals.append(m)
        onehots.append(sel.astype(F32))
        cur = jnp.where(sel, -jnp.inf, cur)
    exps = [jnp.exp(vv - vals[0]) for vv in vals]
    den = exps[0] + exps[1] + exps[2] + exps[3]
    for kk in range(TOP_K):
        gate_ref[kk:kk + 1, :] = exps[kk] / den

    trow = lax.broadcasted_iota(I32, (tm, tm), 0)
    tcol = lax.broadcasted_iota(I32, (tm, tm), 1)
    tri = (trow <= tcol).astype(BF16)
    ohs = jnp.concatenate(onehots, axis=0).astype(BF16)
    pref = jnp.dot(ohs, tri, preferred_element_type=F32)
    base = carry_ref[...]
    for kk in range(TOP_K):
        pk = pref[kk * N_EXPERTS:(kk + 1) * N_EXPERTS]
        rk = jnp.sum(onehots[kk] * (pk - 1.0 + base), axis=0, keepdims=True)
        rank_ref[kk:kk + 1, :] = rk.astype(I32)
        base = base + jnp.sum(onehots[kk], axis=1, keepdims=True)
    carry_ref[...] = base
    cnt_ref[...] = base


def _mixout(yn, bonus, g, yconv, x2, lnw, lnb, wout_bf, nfw, rwt, rb):
    t, d = x2.shape
    tm = min(t, 512)
    row = lambda i: (i, 0)
    const = lambda i: (0, 0)
    colb = lambda i: (0, i)
    half = pl.BlockSpec((tm, RWKV_WIDTH), row)
    full = lambda arr: pl.BlockSpec(arr.shape, const)
    return pl.pallas_call(
        _mixout_body,
        grid=(t // tm,),
        in_specs=[half, half, half, half, pl.BlockSpec((tm, d), row),
                  full(lnw), full(lnb), full(wout_bf), full(nfw), full(rwt), full(rb)],
        out_specs=[pl.BlockSpec((tm, d), row), pl.BlockSpec((tm * (d // LANES), LANES), row),
                   pl.BlockSpec((TOP_K, tm), colb), pl.BlockSpec((TOP_K, tm), colb),
                   pl.BlockSpec((TOP_K, tm), colb), pl.BlockSpec((N_EXPERTS, 1), const)],
        out_shape=[jax.ShapeDtypeStruct((t, d), F32),
                   jax.ShapeDtypeStruct((t * (d // LANES), LANES), F32),
                   jax.ShapeDtypeStruct((TOP_K, t), I32), jax.ShapeDtypeStruct((TOP_K, t), F32),
                   jax.ShapeDtypeStruct((TOP_K, t), I32),
                   jax.ShapeDtypeStruct((N_EXPERTS, 1), F32)],
        scratch_shapes=[pltpu.VMEM((N_EXPERTS, 1), F32)],
        compiler_params=_params(("arbitrary",)),
        name="mix_out",
    )(yn, bonus, g, yconv, x2, lnw, lnb, wout_bf, nfw, rwt, rb)


class _RowRing:
    def __init__(self, idx_hbm, src_hbm, idx_smem, rows, isem, gsem):
        self.idx_hbm, self.src_hbm, self.idx_smem, self.rows = idx_hbm, src_hbm, idx_smem, rows
        self.isem, self.gsem = isem, gsem
        self.depth, _, self.n = idx_smem.shape
        self.ahead = self.depth - 1
        self.c = rows.shape[1] // self.n

    def idx_copy(self, blk):
        slot = blk % self.depth
        return pltpu.make_async_copy(self.idx_hbm.at[blk], self.idx_smem.at[slot],
                                     self.isem.at[slot])

    def row_copy(self, slot, src_row, r):
        c = self.c
        first = src_row * c if isinstance(src_row, int) else pl.multiple_of(src_row * c, c)
        return pltpu.make_async_copy(self.src_hbm.at[pl.ds(first, c)],
                                     self.rows.at[slot, pl.ds(r * c, c)], self.gsem.at[slot])

    def issue_rows(self, blk, unrolled):
        slot = blk % self.depth
        if unrolled:
            for r in range(self.n):
                self.row_copy(slot, self.idx_smem[slot, 0, r], r).start(priority=r % 2)
        else:
            def body(r, carry):
                self.row_copy(slot, self.idx_smem[slot, 0, r], r).start()
                return carry
            lax.fori_loop(0, self.n, body, 0, unroll=8)

    def wait_rows(self, blk):
        slot = blk % self.depth
        for r in range(self.n):
            self.row_copy(slot, 0, r).wait()

    def prologue(self):
        for b in range(self.ahead):
            first = self.idx_copy(b)
            first.start()
            first.wait()
            self.issue_rows(b, unrolled=False)
        self.idx_copy(self.ahead).start()

    def start_next_idx(self, i, last):
        @pl.when(i + self.depth <= last)
        def _():
            self.idx_copy(i + self.depth).start()

    def request(self, i):
        self.idx_copy(i + self.ahead).wait()
        self.issue_rows(i + self.ahead, unrolled=True)


def _dispatch_body(pends_ref, pcounts_ref, dest_hbm, xn_hbm, xs_hbm, dsm, zeros, isem, rsem, zsem):
    i = pl.program_id(0)
    nt = pl.num_programs(0)
    n = dsm.shape[2]
    tm = n // TOP_K
    zrows = zeros.shape[0]
    c = zrows // ROW_BLOCK

    def tile_copy(b):
        return pltpu.make_async_copy(dest_hbm.at[b], dsm.at[b % 2], isem.at[b % 2])

    def row_copy(slot, tok, dst):
        t0 = tok * c if isinstance(tok, int) else pl.multiple_of(tok * c, c)
        d0 = dst * c if isinstance(dst, int) else pl.multiple_of(dst * c, c)
        return pltpu.make_async_copy(xn_hbm.at[pl.ds(t0, c)], xs_hbm.at[pl.ds(d0, c)],
                                     rsem.at[slot])

    def zero_copy(first_row):
        first = (first_row * c if isinstance(first_row, int)
                 else pl.multiple_of(first_row * c, c))
        return pltpu.make_async_copy(zeros, xs_hbm.at[pl.ds(first, zrows)], zsem)

    @pl.when(i == 0)
    def _():
        tile_copy(0).start()
        zeros[...] = jnp.zeros(zeros.shape, F32)
        for e in range(N_EXPERTS):
            @pl.when(pcounts_ref[e] > 0)
            def _():
                zero_copy(pends_ref[e] - ROW_BLOCK).start()
        n_live = pends_ref[N_EXPERTS - 1] // ROW_BLOCK
        n_blocks = xs_hbm.shape[0] // zrows

        def start_dead(b, carry):
            zero_copy(b * ROW_BLOCK).start()
            return carry

        def wait_dead(b, carry):
            zero_copy(b * ROW_BLOCK).wait()
            return carry

        lax.fori_loop(n_live, n_blocks, start_dead, 0)
        for e in range(N_EXPERTS):
            @pl.when(pcounts_ref[e] > 0)
            def _():
                zero_copy(0).wait()
        lax.fori_loop(n_live, n_blocks, wait_dead, 0)

    @pl.when(i + 1 < nt)
    def _():
        tile_copy(i + 1).start()

    tile_copy(i).wait()
    slot = i % 2
    first_tok = i * tm
    for r in range(n):
        row_copy(slot, first_tok + (r % tm), dsm[slot, 0, r]).start(priority=r % 2)

    @pl.when(i > 0)
    def _():
        for r in range(n):
            row_copy(1 - slot, 0, 0).wait()

    @pl.when(i == nt - 1)
    def _():
        for r in range(n):
            row_copy(slot, 0, 0).wait()


def _dispatch(pends, pcounts, dest_tiles, xn2, n_rows):
    nt, _, n = dest_tiles.shape
    c = xn2.shape[0] * TOP_K // (nt * n)
    grid_spec = pltpu.PrefetchScalarGridSpec(
        num_scalar_prefetch=2,
        grid=(nt,),
        in_specs=[pl.BlockSpec(memory_space=pl.ANY), pl.BlockSpec(memory_space=pl.ANY)],
        out_specs=pl.BlockSpec(memory_space=pl.ANY),
        scratch_shapes=[pltpu.SMEM((2, 1, n), I32), pltpu.VMEM((ROW_BLOCK * c, LANES), F32),
                        pltpu.SemaphoreType.DMA((2,)), pltpu.SemaphoreType.DMA((2,)),
                        pltpu.SemaphoreType.DMA],
    )
    return pl.pallas_call(
        _dispatch_body,
        grid_spec=grid_spec,
        out_shape=jax.ShapeDtypeStruct((n_rows * c, LANES), F32),
        compiler_params=_params(("arbitrary",)),
        name="dispatch",
    )(pends, pcounts, dest_tiles, xn2)


def _expert_body(blk_e_ref, nused_ref, xs_ref, gu_ref, gub_ref, dn_ref, dnb_ref, perm_ref,
                 y_ref, wgl, wd):
    i = pl.program_id(0)
    n_live = nused_ref[0]
    grp = perm_ref.shape[0]
    half = grp // 2
    c = dn_ref.shape[1] // LANES
    bm = xs_ref.shape[0] // c

    new_expert = (i == 0) | (blk_e_ref[i] != blk_e_ref[jnp.maximum(i - 1, 0)])

    @pl.when((i < n_live) & new_expert)
    def _():
        for g in range(gu_ref.shape[1] // grp):
            cols = slice(g * grp, (g + 1) * grp)
            wgl[:, cols] = jnp.dot(gu_ref[:, cols].astype(BF16), perm_ref[...],
                                   preferred_element_type=F32).astype(BF16)
        wd[...] = dn_ref[...].astype(BF16)

    @pl.when(i < n_live)
    def _():
        xb = _load_row_tiles(xs_ref, bm, c).astype(BF16)
        h = jnp.dot(xb, wgl[...], preferred_element_type=F32) + gub_ref[...]
        acts = []
        for g in range(h.shape[1] // grp):
            x_glu = jnp.minimum(h[:, g * grp:g * grp + half], SWIGLU_LIMIT)
            x_lin = jnp.clip(h[:, g * grp + half:(g + 1) * grp], -SWIGLU_LIMIT, SWIGLU_LIMIT)
            acts.append(x_glu * _sigmoid(SWIGLU_ALPHA * x_glu) * (x_lin + 1.0))
        act = jnp.concatenate(acts, axis=1)
        y = jnp.dot(act.astype(BF16), wd[...], preferred_element_type=F32) + dnb_ref[...]
        _store_row_tiles(y_ref, y)


    @pl.when(i >= n_live)
    def _():
        y_ref[...] = jnp.zeros(y_ref.shape, F32)


def _experts(blk_e, nused, xs, gu_w, gu_b, dn_w, dn_b, perm):
    bm = ROW_BLOCK
    f2 = gu_w.shape[2]
    f, d = dn_w.shape[1:]
    c = d // LANES
    nb = xs.shape[0] // (bm * c)
    wspec = lambda shape: pl.BlockSpec((None,) + shape, lambda i, be, nu: (be[i], 0, 0))
    grid_spec = pltpu.PrefetchScalarGridSpec(
        num_scalar_prefetch=2,
        grid=(nb,),
        in_specs=[pl.BlockSpec((bm * c, LANES), lambda i, be, nu: (jnp.minimum(i, nu[0] - 1), 0)),
                  wspec((d, f2)), wspec((1, f2)), wspec((f, d)), wspec((1, d)),
                  pl.BlockSpec(perm.shape, lambda i, be, nu: (0, 0))],
        out_specs=pl.BlockSpec((bm * c, LANES), lambda i, be, nu: (i, 0)),
        scratch_shapes=[pltpu.VMEM((d, f2), BF16), pltpu.VMEM((f, d), BF16)],
    )
    return pl.pallas_call(
        _expert_body,
        grid_spec=grid_spec,
        out_shape=jax.ShapeDtypeStruct((nb * bm * c, LANES), F32),
        compiler_params=_params(("arbitrary",)),
        name="experts",
    )(blk_e, nused, xs, gu_w, gu_b, dn_w, dn_b, perm)


def _combine_body(dest_hbm, yb_hbm, h_ref, gate_ref, nw_ref, o_ref, ybuf, dest_smem, isem, gsem):
    i = pl.program_id(0)
    nt = pl.num_programs(0)
    tm = h_ref.shape[0]
    ring = _RowRing(dest_hbm, yb_hbm, dest_smem, ybuf, isem, gsem)

    @pl.when(i == 0)
    def _():
        ring.prologue()

    ring.start_next_idx(i, nt - 1)

    @pl.when(i + ring.ahead <= nt - 1)
    def _():
        ring.request(i)

    ring.wait_rows(i)
    rows = ybuf.at[i % ring.depth]
    c = ring.c
    gates = gate_ref[...]
    acc = h_ref[...]
    for kk in range(TOP_K):
        acc = acc + gates[:, kk:kk + 1] * _load_row_tiles(rows, tm, c, first=kk * tm * c)
    o_ref[...] = _rms(acc, nw_ref[...])


def _combine(dest_tiles, yb, h, gates_t, nfw):
    t, d = h.shape
    nt, _, n = dest_tiles.shape
    tm = n // TOP_K
    row = lambda i: (i, 0)
    return pl.pallas_call(
        _combine_body,
        grid=(nt,),
        in_specs=[pl.BlockSpec(memory_space=pl.ANY), pl.BlockSpec(memory_space=pl.ANY),
                  pl.BlockSpec((tm, d), row), pl.BlockSpec((tm, TOP_K), row),
                  pl.BlockSpec((1, d), lambda i: (0, 0))],
        out_specs=pl.BlockSpec((tm, d), row),
        out_shape=jax.ShapeDtypeStruct((t, d), F32),
        scratch_shapes=[pltpu.VMEM((2, n * (d // LANES), LANES), F32), pltpu.SMEM((2, 1, n), I32),
                        pltpu.SemaphoreType.DMA((2,)), pltpu.SemaphoreType.DMA((2,))],
        compiler_params=_params(("arbitrary",)),
        name="combine",
    )(dest_tiles, yb, h, gates_t, nfw)


def _head_ones():
    hi = lax.broadcasted_iota(I32, (RWKV_WIDTH, RWKV_WIDTH), 0) // HEAD
    hj = lax.broadcasted_iota(I32, (RWKV_WIDTH, RWKV_WIDTH), 1) // HEAD
    return (hi == hj).astype(BF16)


def _layer(x2, bsz, seq, norm_mix_w, w_in, conv_w, mu_shift, w0, w_decay_up, a0, w_iclr_up,
           w_gate_up, k_k, k_a, r_k, ln_x_w, ln_x_b, w_out, norm_ffn_w, router_w, router_b,
           gu_w, gu_b, dn_w, dn_b):
    t, d = x2.shape
    row1 = lambda a: a.reshape(1, -1)

    p = _inproj(x2, row1(norm_mix_w), w_in.astype(BF16))
    yconv, r, k, v, lw, a, b, g, bonus = _prep(
        p, bsz, seq, conv_w, row1(mu_shift), row1(w0), w_decay_up, row1(a0), w_iclr_up,
        w_gate_up, row1(k_k), row1(k_a), row1(r_k), _head_ones())
    yn = _scan(r, k, v, lw, a, b, bsz, seq)
    h, xn2, idx, gates, rank, cnt = _mixout(
        yn, bonus, g, yconv, x2, row1(ln_x_w), row1(ln_x_b), w_out.astype(BF16),
        row1(norm_ffn_w), router_w.T, router_b.reshape(-1, 1))

    bm = ROW_BLOCK
    counts = cnt[:, 0].astype(I32)
    pcounts = (counts + bm - 1) // bm * bm
    pends = jnp.cumsum(pcounts)
    pstarts = pends - pcounts
    eids = jnp.arange(N_EXPERTS, dtype=I32)
    dest = rank + jnp.sum(jnp.where(idx[None] == eids[:, None, None], pstarts[:, None, None], 0),
                          axis=0)
    n_blocks = t * TOP_K // bm + N_EXPERTS
    n_rows = n_blocks * bm
    tm = min(t, 256)
    nt = t // tm
    assert nt >= 2
    dest_tiles = dest.reshape(TOP_K, nt, tm).transpose(1, 0, 2).reshape(nt, 1, TOP_K * tm)
    xs = _dispatch(pends, pcounts, dest_tiles, xn2, n_rows)
    blk_start = jnp.arange(n_blocks, dtype=I32) * bm
    blk_e = jnp.minimum(jnp.sum((pends[None, :] <= blk_start[:, None]).astype(I32), axis=1),
                        N_EXPERTS - 1)
    nused = (pends[-1] // bm).astype(I32).reshape(1)

    grp = 2 * LANES
    f2 = gu_w.shape[2]
    src = jnp.arange(grp, dtype=I32)[:, None]
    dst = jnp.arange(grp, dtype=I32)[None, :]
    perm = (src == jnp.where(dst < LANES, 2 * dst, 2 * (dst - LANES) + 1)).astype(BF16)
    gub = gu_b.reshape(N_EXPERTS, f2 // grp, LANES, 2).transpose(0, 1, 3, 2).reshape(N_EXPERTS, 1, f2)
    yb = _experts(blk_e, nused, xs, gu_w, gub, dn_w, dn_b.reshape(N_EXPERTS, 1, d), perm)

    return dest_tiles, yb, h, gates.T


def kernel(x, norm_mix_w, w_in, conv_w, mu_shift, w0, w_decay_up, a0, w_iclr_up, w_gate_up,
           k_k, k_a, r_k, ln_x_w, ln_x_b, w_out, norm_ffn_w, router_w, router_b,
           gu_w, gu_b, dn_w, dn_b, norm_final_w):
    bsz, seq, d = x.shape
    assert w_in.shape[0] == 1, "the final RMSNorm is fused into the single layer's combine step"
    dest_tiles, yb, h_mid, gates_t = _layer(
        x.reshape(bsz * seq, d), bsz, seq, norm_mix_w[0], w_in[0], conv_w[0], mu_shift[0], w0[0],
        w_decay_up[0], a0[0], w_iclr_up[0], w_gate_up[0], k_k[0], k_a[0], r_k[0], ln_x_w[0],
        ln_x_b[0], w_out[0], norm_ffn_w[0], router_w[0], router_b[0], gu_w[0], gu_b[0], dn_w[0],
        dn_b[0])
    out = _combine(dest_tiles, yb, h_mid, gates_t, norm_final_w.reshape(1, d))
    return out.reshape(bsz, seq, d)
```

```python
import functools

import jax
import jax.numpy as jnp
from jax import lax
from jax.experimental import pallas as pl
from jax.experimental.pallas import tpu as pltpu

F32 = jnp.float32
BF16 = jnp.bfloat16
I32 = jnp.int32

CONV_WIDTH = 512
CONV_K = 3
RWKV_WIDTH = 512
HEAD = 64
HEADS = RWKV_WIDTH // HEAD
LORA_DECAY = 64
LORA_ICLR = 64
LORA_GATE = 128
N_EXPERTS = 32
TOP_K = 4
SWIGLU_LIMIT = 7.0
SWIGLU_ALPHA = 1.702
RMS_EPS = 1e-5
GN_EPS = 64e-5

CHUNK = 64
SUB = 16
SCAN_CHUNKS = 2
ROW_BLOCK = 512
LANES = 128
VMEM_LIMIT = 56 * 1024 * 1024


def _params(sem, vmem=VMEM_LIMIT):
    return pltpu.CompilerParams(dimension_semantics=sem, vmem_limit_bytes=vmem)


def _mm(a, b):
    return jnp.dot(a.astype(BF16), b.astype(BF16), preferred_element_type=F32)


def _mm_nt(a, b):
    return lax.dot_general(a.astype(BF16), b.astype(BF16), (((1,), (1,)), ((), ())),
                           preferred_element_type=F32)


def _mm_tn(a, b):
    return lax.dot_general(a.astype(BF16), b.astype(BF16), (((0,), (0,)), ((), ())),
                           preferred_element_type=F32)


def _split_dot(x, m01, terms, left):
    acc = None
    rem = x
    for _ in range(terms):
        piece = rem.astype(BF16)
        rem = rem - piece.astype(F32)
        part = (jnp.dot(m01, piece, preferred_element_type=F32) if left
                else jnp.dot(piece, m01, preferred_element_type=F32))
        acc = part if acc is None else acc + part
    return acc


def _sigmoid(x):
    return 1.0 / (1.0 + jnp.exp(-x))


def _rms(x, w):
    return x * lax.rsqrt(jnp.mean(x * x, axis=-1, keepdims=True) + RMS_EPS) * w


def _store_row_tiles(ref, x):
    n, width = x.shape
    c = width // LANES
    for j in range(c):
        ref[pl.ds(j, n, stride=c), :] = x[:, j * LANES:(j + 1) * LANES]


def _load_row_tiles(ref, n, c, first=0):
    return jnp.concatenate([ref[pl.ds(first + j, n, stride=c), :] for j in range(c)], axis=1)


def _inproj_body(x_ref, nw_ref, w_ref, p_ref):
    xn = _rms(x_ref[...], nw_ref[...])
    p_ref[...] = jnp.dot(xn.astype(BF16), w_ref[...], preferred_element_type=F32)


def _inproj(x2, nw, w_bf):
    t, d = x2.shape
    c = w_bf.shape[1]
    tm = min(t, 512)
    return pl.pallas_call(
        _inproj_body,
        grid=(t // tm,),
        in_specs=[pl.BlockSpec((tm, d), lambda i: (i, 0)),
                  pl.BlockSpec((1, d), lambda i: (0, 0)),
                  pl.BlockSpec((d, c), lambda i: (0, 0))],
        out_specs=pl.BlockSpec((tm, c), lambda i: (i, 0)),
        out_shape=jax.ShapeDtypeStruct((t, c), F32),
        compiler_params=_params(("parallel",)),
        name="inproj",
    )(x2, nw, w_bf)


def _prep_body(p_ref, convw_ref, mu_ref, w0_ref, wdu_ref, a0_ref, wiu_ref, wgu_ref,
               kk_ref, ka_ref, rk_ref, ones_ref,
               yconv_ref, r_ref, k_ref, v_ref, lw_ref, a_ref, b_ref, g_ref, bonus_ref,
               buf_ref):
    ts = p_ref.shape[0]
    cw, rw = CONV_WIDTH, RWKV_WIDTH
    o_r = 3 * cw

    @pl.when(pl.program_id(1) == 0)
    def _():
        buf_ref[0:8, :] = jnp.zeros((8, buf_ref.shape[1]), F32)

    buf_ref[8:8 + ts, :] = p_ref[...]

    u0 = buf_ref[8:8 + ts, cw:2 * cw] * buf_ref[8:8 + ts, 2 * cw:3 * cw]
    u1 = buf_ref[7:7 + ts, cw:2 * cw] * buf_ref[7:7 + ts, 2 * cw:3 * cw]
    u2 = buf_ref[6:6 + ts, cw:2 * cw] * buf_ref[6:6 + ts, 2 * cw:3 * cw]
    conv = convw_ref[0:1, :] * u2 + convw_ref[1:2, :] * u1 + convw_ref[2:3, :] * u0
    yconv_ref[...] = buf_ref[8:8 + ts, 0:cw] * conv

    def shifted(lo, hi):
        cur = buf_ref[8:8 + ts, o_r + lo:o_r + hi]
        prev = buf_ref[7:7 + ts, o_r + lo:o_r + hi]
        return cur + (prev - cur) * mu_ref[:, lo:hi]

    r = shifted(0, rw)
    k = shifted(rw, 2 * rw)
    v = shifted(2 * rw, 3 * rw)
    o1 = 3 * rw
    d_w = shifted(o1, o1 + LORA_DECAY)
    d_a = shifted(o1 + LORA_DECAY, o1 + LORA_DECAY + LORA_ICLR)
    d_g = shifted(o1 + LORA_DECAY + LORA_ICLR, o1 + LORA_DECAY + LORA_ICLR + LORA_GATE)

    z = -(w0_ref[...] + _mm(jnp.tanh(d_w), wdu_ref[...]))
    softplus = jnp.maximum(z, 0.0) + jnp.log(1.0 + jnp.exp(-jnp.abs(z)))
    w_log = -softplus - 0.5
    lw_ref[...] = -jnp.exp(w_log)
    iclr = _sigmoid(a0_ref[...] + _mm(d_a, wiu_ref[...]))
    g_ref[...] = _mm(_sigmoid(d_g), wgu_ref[...])

    ones = ones_ref[...]
    kk = k * kk_ref[...]
    nrm = jnp.sqrt(_split_dot(kk * kk, ones, 2, left=False))
    kk = kk / jnp.maximum(nrm, 1e-12)
    k2 = k * (1.0 + (iclr - 1.0) * ka_ref[...])
    r_ref[...] = r
    k_ref[...] = k2
    v_ref[...] = v
    a_ref[...] = -kk
    b_ref[...] = kk * iclr
    bonus_ref[...] = _split_dot(r * k2 * rk_ref[...], ones, 2, left=False) * v

    buf_ref[0:8, :] = buf_ref[ts:ts + 8, :]


def _prep(p, bsz, seq, convw, mu, w0, wdu, a0, wiu, wgu, kkw, kaw, rkw, ones):
    t, c = p.shape
    ts = min(seq, 256)
    ns = seq // ts
    row = lambda b, s: (b * ns + s, 0)
    const = lambda b, s: (0, 0)
    full = lambda arr: pl.BlockSpec(arr.shape, const)
    out_spec = pl.BlockSpec((ts, RWKV_WIDTH), row)
    out_sds = jax.ShapeDtypeStruct((t, RWKV_WIDTH), F32)
    return pl.pallas_call(
        _prep_body,
        grid=(bsz, ns),
        in_specs=[pl.BlockSpec((ts, c), row)] + [full(a) for a in
                  (convw, mu, w0, wdu, a0, wiu, wgu, kkw, kaw, rkw, ones)],
        out_specs=[out_spec] * 9,
        out_shape=[out_sds] * 9,
        scratch_shapes=[pltpu.VMEM((ts + 8, c), F32)],
        compiler_params=_params(("parallel", "arbitrary")),
        name="prep",
    )(p, convw, mu, w0, wdu, a0, wiu, wgu, kkw, kaw, rkw, ones)


def _unit_lower_inverse(p_low, eye, bd_mask):
    pd = [jnp.where(bd_mask, p, 0.0) for p in p_low]
    po = [p - x for p, x in zip(p_low, pd)]
    d = [eye + x for x in pd]
    pk = pd
    for _ in range(3):
        pk = [_mm(x, x) for x in pk]
        dd = [_mm(x, y) for x, y in zip(d, pk)]
        d = [x + y for x, y in zip(d, dd)]
    q = [_mm(x, y) for x, y in zip(d, po)]
    q2 = [_mm(x, x) for x in q]
    q3 = [_mm(x, y) for x, y in zip(q, q2)]
    w = [eye + a + b + c for a, b, c in zip(q, q2, q3)]
    return [_mm(x, y) for x, y in zip(w, d)]


def _scan_body(r_ref, k_ref, v_ref, lw_ref, a_ref, b_ref, y_ref, s_ref):
    L = CHUNK

    @pl.when(pl.program_id(1) == 0)
    def _():
        s_ref[...] = jnp.zeros(s_ref.shape, F32)

    row = lax.broadcasted_iota(I32, (L, L), 0)
    col = lax.broadcasted_iota(I32, (L, L), 1)
    lower = row >= col
    strict = row > col
    eye = (row == col).astype(F32)
    bd_mask = (row // SUB) == (col // SUB)

    rows = lw_ref.shape[0]
    nsub = rows // L
    rr = lax.broadcasted_iota(I32, (rows, rows), 0)
    cc = lax.broadcasted_iota(I32, (rows, rows), 1)
    chunk_lower = ((rr >= cc) & (rr // L == cc // L)).astype(BF16)
    lw = lw_ref[...]
    c = _split_dot(lw, chunk_lower, 3, left=True)
    e_in = jnp.exp(c)
    e_inv = jnp.exp(-c)
    e_ex = jnp.exp(c - lw)
    a_t = a_ref[...] * e_ex
    r_t = r_ref[...] * e_in
    b_t = b_ref[...] * e_inv
    k_t = k_ref[...] * e_inv
    v = v_ref[...]

    hs = range(HEADS)
    sls = [slice(h * HEAD, (h + 1) * HEAD) for h in hs]
    items = [(slice(s * L, (s + 1) * L), sl) for s in range(nsub) for sl in sls]
    g_last = [e_in[(s + 1) * L - 1:(s + 1) * L, :] for s in range(nsub)]
    ar = [jnp.concatenate([a_t[ts, sl], r_t[ts, sl]], axis=0) for ts, sl in items]
    gb = [_mm_nt(x, b_t[ts, sl]) for x, (ts, sl) in zip(ar, items)]
    gk = [_mm_nt(x, k_t[ts, sl]) for x, (ts, sl) in zip(ar, items)]
    p_low = [jnp.where(strict, x[0:L], 0.0) for x in gb]
    m_br = [jnp.where(lower, x[L:2 * L], 0.0) for x in gb]
    m_akr = [jnp.concatenate([jnp.where(strict, x[0:L], 0.0), jnp.where(lower, x[L:2 * L], 0.0)],
                             axis=0) for x in gk]
    vh = [v[ts, sl] for ts, sl in items]
    mv = [_mm(x, y) for x, y in zip(m_akr, vh)]
    t_inv = _unit_lower_inverse(p_low, eye, bd_mask)

    state = [s_ref[h] for h in hs]
    for s in range(nsub):
        it = [s * HEADS + h for h in hs]
        ts = items[it[0]][0]
        ars = [_mm_nt(ar[j], state[h]) for h, j in zip(hs, it)]
        u = [_mm(t_inv[j], ars[h][0:L] + mv[j][0:L]) for h, j in zip(hs, it)]
        mu = [_mm(m_br[j], u[h]) for h, j in zip(hs, it)]
        s_add = [_mm_tn(jnp.concatenate([u[h], vh[j]], axis=0),
                        jnp.concatenate([b_t[ts, sls[h]], k_t[ts, sls[h]]], axis=0)
                        * g_last[s][:, sls[h]]) for h, j in zip(hs, it)]
        for h, j in zip(hs, it):
            state[h] = state[h] * g_last[s][:, sls[h]] + s_add[h]
            y = ars[h][L:2 * L] + mu[h] + mv[j][L:2 * L]
            mean = jnp.mean(y, axis=-1, keepdims=True)
            yc = y - mean
            var = jnp.mean(yc * yc, axis=-1, keepdims=True)
            y_ref[ts, sls[h]] = yc * lax.rsqrt(var + GN_EPS)
    for h in hs:
        s_ref[h] = state[h]


def _scan(r, k, v, lw, a, b, bsz, seq):
    t = r.shape[0]
    rows = CHUNK * SCAN_CHUNKS
    nc = seq // rows
    spec = pl.BlockSpec((rows, RWKV_WIDTH), lambda bi, ci: (bi * nc + ci, 0))
    return pl.pallas_call(
        _scan_body,
        grid=(bsz, nc),
        in_specs=[spec] * 6,
        out_specs=spec,
        out_shape=jax.ShapeDtypeStruct((t, RWKV_WIDTH), F32),
        scratch_shapes=[pltpu.VMEM((HEADS, HEAD, HEAD), F32)],
        compiler_params=_params(("parallel", "arbitrary")),
        name="scan",
    )(r, k, v, lw, a, b)


def _mixout_body(yn_ref, bonus_ref, g_ref, yconv_ref, x_ref, lnw_ref, lnb_ref, wout_ref,
                 nfw_ref, rwt_ref, rb_ref,
                 h_ref, xn_ref, idx_ref, gate_ref, rank_ref, cnt_ref, carry_ref):
    tm = x_ref.shape[0]

    @pl.when(pl.program_id(0) == 0)
    def _():
        carry_ref[...] = jnp.zeros(carry_ref.shape, F32)

    y_rwkv = (yn_ref[...] * lnw_ref[...] + lnb_ref[...] + bonus_ref[...]) * g_ref[...]
    cat = jnp.concatenate([yconv_ref[...], y_rwkv], axis=1)
    h = x_ref[...] + jnp.dot(cat.astype(BF16), wout_ref[...], preferred_element_type=F32)
    h_ref[...] = h
    xn = _rms(h, nfw_ref[...])
    _store_row_tiles(xn_ref, xn)

    lt = lax.dot_general(rwt_ref[...], xn, (((1,), (1,)), ((), ())),
                         precision=lax.Precision.HIGHEST, preferred_element_type=F32)
    lt = lt + rb_ref[...]
    eio = lax.broadcasted_iota(I32, lt.shape, 0)
    vals, onehots = [], []
    cur = lt
    for kk in range(TOP_K):
        m = jnp.max(cur, axis=0, keepdims=True)
        ix = jnp.min(jnp.where(cur == m, eio, N_EXPERTS), axis=0, keepdims=True)
        sel = eio == ix
        idx_ref[kk:kk + 1, :] = ix
        vals.append(m)
        onehots.append(sel.astype(F32))
        cur = jnp.where(sel, -jnp.inf, cur)
    exps = [jnp.exp(vv - vals[0]) for vv in vals]
    den = exps[0] + exps[1] + exps[2] + exps[3]
    for kk in range(TOP_K):
        gate_ref[kk:kk + 1, :] = exps[kk] / den

    trow = lax.broadcasted_iota(I32, (tm, tm), 0)
    tcol = lax.broadcasted_iota(I32, (tm, tm), 1)
    tri = (trow <= tcol).astype(BF16)
    ohs = jnp.concatenate(onehots, axis=0).astype(BF16)
    pref = jnp.dot(ohs, tri, preferred_element_type=F32)
    base = carry_ref[...]
    for kk in range(TOP_K):
        pk = pref[kk * N_EXPERTS:(kk + 1) * N_EXPERTS]
        rk = jnp.sum(onehots[kk] * (pk - 1.0 + base), axis=0, keepdims=True)
        rank_ref[kk:kk + 1, :] = rk.astype(I32)
        base = base + jnp.sum(onehots[kk], axis=1, keepdims=True)
    carry_ref[...] = base
    cnt_ref[...] = base


def _mixout(yn, bonus, g, yconv, x2, lnw, lnb, wout_bf, nfw, rwt, rb):
    t, d = x2.shape
    tm = min(t, 512)
    row = lambda i: (i, 0)
    const = lambda i: (0, 0)
    colb = lambda i: (0, i)
    half = pl.BlockSpec((tm, RWKV_WIDTH), row)
    full = lambda arr: pl.BlockSpec(arr.shape, const)
    return pl.pallas_call(
        _mixout_body,
        grid=(t // tm,),
        in_specs=[half, half, half, half, pl.BlockSpec((tm, d), row),
                  full(lnw), full(lnb), full(wout_bf), full(nfw), full(rwt), full(rb)],
        out_specs=[pl.BlockSpec((tm, d), row), pl.BlockSpec((tm * (d // LANES), LANES), row),
                   pl.BlockSpec((TOP_K, tm), colb), pl.BlockSpec((TOP_K, tm), colb),
                   pl.BlockSpec((TOP_K, tm), colb), pl.BlockSpec((N_EXPERTS, 1), const)],
        out_shape=[jax.ShapeDtypeStruct((t, d), F32),
                   jax.ShapeDtypeStruct((t * (d // LANES), LANES), F32),
                   jax.ShapeDtypeStruct((TOP_K, t), I32), jax.ShapeDtypeStruct((TOP_K, t), F32),
                   jax.ShapeDtypeStruct((TOP_K, t), I32),
                   jax.ShapeDtypeStruct((N_EXPERTS, 1), F32)],
        scratch_shapes=[pltpu.VMEM((N_EXPERTS, 1), F32)],
        compiler_params=_params(("arbitrary",)),
        name="mix_out",
    )(yn, bonus, g, yconv, x2, lnw, lnb, wout_bf, nfw, rwt, rb)


class _RowRing:
    def __init__(self, idx_hbm, src_hbm, idx_smem, rows, isem, gsem):
        self.idx_hbm, self.src_hbm, self.idx_smem, self.rows = idx_hbm, src_hbm, idx_smem, rows
        self.isem, self.gsem = isem, gsem
        self.depth, _, self.n = idx_smem.shape
        self.ahead = self.depth - 1
        self.c = rows.shape[1] // self.n

    def idx_copy(self, blk):
        slot = blk % self.depth
        return pltpu.make_async_copy(self.idx_hbm.at[blk], self.idx_smem.at[slot],
                                     self.isem.at[slot])

    def row_copy(self, slot, src_row, r):
        c = self.c
        first = src_row * c if isinstance(src_row, int) else pl.multiple_of(src_row * c, c)
        return pltpu.make_async_copy(self.src_hbm.at[pl.ds(first, c)],
                                     self.rows.at[slot, pl.ds(r * c, c)], self.gsem.at[slot])

    def issue_rows(self, blk, unrolled):
        slot = blk % self.depth
        if unrolled:
            for r in range(self.n):
                self.row_copy(slot, self.idx_smem[slot, 0, r], r).start(priority=r % 2)
        else:
            def body(r, carry):
                self.row_copy(slot, self.idx_smem[slot, 0, r], r).start()
                return carry
            lax.fori_loop(0, self.n, body, 0, unroll=8)

    def wait_rows(self, blk):
        slot = blk % self.depth
        for r in range(self.n):
            self.row_copy(slot, 0, r).wait()

    def prologue(self):
        for b in range(self.ahead):
            first = self.idx_copy(b)
            first.start()
            first.wait()
            self.issue_rows(b, unrolled=False)
        self.idx_copy(self.ahead).start()

    def start_next_idx(self, i, last):
        @pl.when(i + self.depth <= last)
        def _():
            self.idx_copy(i + self.depth).start()

    def request(self, i):
        self.idx_copy(i + self.ahead).wait()
        self.issue_rows(i + self.ahead, unrolled=True)


def _dispatch_body(pends_ref, pcounts_ref, dest_hbm, xn_ref, xs_hbm, dsm, stage, zeros, isem, rsem,
                   zsem):
    i = pl.program_id(0)
    nt = pl.num_programs(0)
    n = dsm.shape[2]
    tm = n // TOP_K
    zrows = zeros.shape[0]
    c = zrows // ROW_BLOCK

    def tile_copy(b):
        return pltpu.make_async_copy(dest_hbm.at[b], dsm.at[b % 2], isem.at[b % 2])

    def row_copy(slot, tl, dst):
        d0 = dst * c if isinstance(dst, int) else pl.multiple_of(dst * c, c)
        return pltpu.make_async_copy(stage.at[slot, pl.ds(tl * c, c)], xs_hbm.at[pl.ds(d0, c)],
                                     rsem.at[slot])

    def zero_copy(first_row):
        first = (first_row * c if isinstance(first_row, int)
                 else pl.multiple_of(first_row * c, c))
        return pltpu.make_async_copy(zeros, xs_hbm.at[pl.ds(first, zrows)], zsem)

    @pl.when(i == 0)
    def _():
        tile_copy(0).start()
        zeros[...] = jnp.zeros(zeros.shape, F32)
        for e in range(N_EXPERTS):
            @pl.when(pcounts_ref[e] > 0)
            def _():
                zero_copy(pends_ref[e] - ROW_BLOCK).start()
        n_live = pends_ref[N_EXPERTS - 1] // ROW_BLOCK
        n_blocks = xs_hbm.shape[0] // zrows

        def start_dead(b, carry):
            zero_copy(b * ROW_BLOCK).start()
            return carry

        def wait_dead(b, carry):
            zero_copy(b * ROW_BLOCK).wait()
            return carry

        lax.fori_loop(n_live, n_blocks, start_dead, 0)
        for e in range(N_EXPERTS):
            @pl.when(pcounts_ref[e] > 0)
            def _():
                zero_copy(0).wait()
        lax.fori_loop(n_live, n_blocks, wait_dead, 0)

    @pl.when(i + 1 < nt)
    def _():
        tile_copy(i + 1).start()

    tile_copy(i).wait()
    slot = i % 2
    stage[slot] = xn_ref[...]
    for r in range(n):
        row_copy(slot, r % tm, dsm[slot, 0, r]).start(priority=r % 2)

    @pl.when(i > 0)
    def _():
        for r in range(n):
            row_copy(1 - slot, 0, 0).wait()

    @pl.when(i == nt - 1)
    def _():
        for r in range(n):
            row_copy(slot, 0, 0).wait()


def _dispatch(pends, pcounts, dest_tiles, xn2, n_rows):
    nt, _, n = dest_tiles.shape
    c = xn2.shape[0] * TOP_K // (nt * n)
    grid_spec = pltpu.PrefetchScalarGridSpec(
        num_scalar_prefetch=2,
        grid=(nt,),
        in_specs=[pl.BlockSpec(memory_space=pl.ANY),
                  pl.BlockSpec((n // TOP_K * c, LANES), lambda i, pe, pc: (i, 0))],
        out_specs=pl.BlockSpec(memory_space=pl.ANY),
        scratch_shapes=[pltpu.SMEM((2, 1, n), I32), pltpu.VMEM((2, n // TOP_K * c, LANES), F32),
                        pltpu.VMEM((ROW_BLOCK * c, LANES), F32),
                        pltpu.SemaphoreType.DMA((2,)), pltpu.SemaphoreType.DMA((2,)),
                        pltpu.SemaphoreType.DMA],
    )
    return pl.pallas_call(
        _dispatch_body,
        grid_spec=grid_spec,
        out_shape=jax.ShapeDtypeStruct((n_rows * c, LANES), F32),
        compiler_params=_params(("arbitrary",)),
        name="dispatch",
    )(pends, pcounts, dest_tiles, xn2)


def _expert_body(blk_e_ref, nused_ref, xs_ref, gu_ref, gub_ref, dn_ref, dnb_ref, perm_ref,
                 y_ref, wgl, wd):
    i = pl.program_id(0)
    n_live = nused_ref[0]
    grp = perm_ref.shape[0]
    half = grp // 2
    c = dn_ref.shape[1] // LANES
    bm = xs_ref.shape[0] // c

    new_expert = (i == 0) | (blk_e_ref[i] != blk_e_ref[jnp.maximum(i - 1, 0)])

    @pl.when((i < n_live) & new_expert)
    def _():
        for g in range(gu_ref.shape[1] // grp):
            cols = slice(g * grp, (g + 1) * grp)
            wgl[:, cols] = jnp.dot(gu_ref[:, cols].astype(BF16), perm_ref[...],
                                   preferred_element_type=F32).astype(BF16)
        wd[...] = dn_ref[...].astype(BF16)

    @pl.when(i < n_live)
    def _():
        xb = _load_row_tiles(xs_ref, bm, c).astype(BF16)
        h = jnp.dot(xb, wgl[...], preferred_element_type=F32) + gub_ref[...]
        acts = []
        for g in range(h.shape[1] // grp):
            x_glu = jnp.minimum(h[:, g * grp:g * grp + half], SWIGLU_LIMIT)
            x_lin = jnp.clip(h[:, g * grp + half:(g + 1) * grp], -SWIGLU_LIMIT, SWIGLU_LIMIT)
            acts.append(x_glu * _sigmoid(SWIGLU_ALPHA * x_glu) * (x_lin + 1.0))
        act = jnp.concatenate(acts, axis=1)
        y = jnp.dot(act.astype(BF16), wd[...], preferred_element_type=F32) + dnb_ref[...]
        _store_row_tiles(y_ref, y)


    @pl.when(i >= n_live)
    def _():
        y_ref[...] = jnp.zeros(y_ref.shape, F32)


def _experts(blk_e, nused, xs, gu_w, gu_b, dn_w, dn_b, perm):
    bm = ROW_BLOCK
    f2 = gu_w.shape[2]
    f, d = dn_w.shape[1:]
    c = d // LANES
    nb = xs.shape[0] // (bm * c)
    wspec = lambda shape: pl.BlockSpec((None,) + shape, lambda i, be, nu: (be[i], 0, 0))
    grid_spec = pltpu.PrefetchScalarGridSpec(
        num_scalar_prefetch=2,
        grid=(nb,),
        in_specs=[pl.BlockSpec((bm * c, LANES), lambda i, be, nu: (jnp.minimum(i, nu[0] - 1), 0)),
                  wspec((d, f2)), wspec((1, f2)), wspec((f, d)), wspec((1, d)),
                  pl.BlockSpec(perm.shape, lambda i, be, nu: (0, 0))],
        out_specs=pl.BlockSpec((bm * c, LANES), lambda i, be, nu: (i, 0)),
        scratch_shapes=[pltpu.VMEM((d, f2), BF16), pltpu.VMEM((f, d), BF16)],
    )
    return pl.pallas_call(
        _expert_body,
        grid_spec=grid_spec,
        out_shape=jax.ShapeDtypeStruct((nb * bm * c, LANES), F32),
        compiler_params=_params(("arbitrary",)),
        name="experts",
    )(blk_e, nused, xs, gu_w, gu_b, dn_w, dn_b, perm)


def _combine_body(dest_hbm, yb_hbm, h_ref, gate_ref, nw_ref, o_ref, ybuf, dest_smem, isem, gsem):
    i = pl.program_id(0)
    nt = pl.num_programs(0)
    tm = h_ref.shape[0]
    ring = _RowRing(dest_hbm, yb_hbm, dest_smem, ybuf, isem, gsem)

    @pl.when(i == 0)
    def _():
        ring.prologue()

    ring.start_next_idx(i, nt - 1)

    @pl.when(i + ring.ahead <= nt - 1)
    def _():
        ring.request(i)

    ring.wait_rows(i)
    rows = ybuf.at[i % ring.depth]
    c = ring.c
    gates = gate_ref[...]
    acc = h_ref[...]
    for kk in range(TOP_K):
        acc = acc + gates[:, kk:kk + 1] * _load_row_tiles(rows, tm, c, first=kk * tm * c)
    o_ref[...] = _rms(acc, nw_ref[...])


def _combine(dest_tiles, yb, h, gates_t, nfw):
    t, d = h.shape
    nt, _, n = dest_tiles.shape
    tm = n // TOP_K
    row = lambda i: (i, 0)
    return pl.pallas_call(
        _combine_body,
        grid=(nt,),
        in_specs=[pl.BlockSpec(memory_space=pl.ANY), pl.BlockSpec(memory_space=pl.ANY),
                  pl.BlockSpec((tm, d), row), pl.BlockSpec((tm, TOP_K), row),
                  pl.BlockSpec((1, d), lambda i: (0, 0))],
        out_specs=pl.BlockSpec((tm, d), row),
        out_shape=jax.ShapeDtypeStruct((t, d), F32),
        scratch_shapes=[pltpu.VMEM((2, n * (d // LANES), LANES), F32), pltpu.SMEM((2, 1, n), I32),
                        pltpu.SemaphoreType.DMA((2,)), pltpu.SemaphoreType.DMA((2,))],
        compiler_params=_params(("arbitrary",)),
        name="combine",
    )(dest_tiles, yb, h, gates_t, nfw)


def _head_ones():
    hi = lax.broadcasted_iota(I32, (RWKV_WIDTH, RWKV_WIDTH), 0) // HEAD
    hj = lax.broadcasted_iota(I32, (RWKV_WIDTH, RWKV_WIDTH), 1) // HEAD
    return (hi == hj).astype(BF16)


def _layer(x2, bsz, seq, norm_mix_w, w_in, conv_w, mu_shift, w0, w_decay_up, a0, w_iclr_up,
           w_gate_up, k_k, k_a, r_k, ln_x_w, ln_x_b, w_out, norm_ffn_w, router_w, router_b,
           gu_w, gu_b, dn_w, dn_b):
    t, d = x2.shape
    row1 = lambda a: a.reshape(1, -1)

    p = _inproj(x2, row1(norm_mix_w), w_in.astype(BF16))
    yconv, r, k, v, lw, a, b, g, bonus = _prep(
        p, bsz, seq, conv_w, row1(mu_shift), row1(w0), w_decay_up, row1(a0), w_iclr_up,
        w_gate_up, row1(k_k), row1(k_a), row1(r_k), _head_ones())
    yn = _scan(r, k, v, lw, a, b, bsz, seq)
    h, xn2, idx, gates, rank, cnt = _mixout(
        yn, bonus, g, yconv, x2, row1(ln_x_w), row1(ln_x_b), w_out.astype(BF16),
        row1(norm_ffn_w), router_w.T, router_b.reshape(-1, 1))

    bm = ROW_BLOCK
    counts = cnt[:, 0].astype(I32)
    pcounts = (counts + bm - 1) // bm * bm
    pends = jnp.cumsum(pcounts)
    pstarts = pends - pcounts
    eids = jnp.arange(N_EXPERTS, dtype=I32)
    dest = rank + jnp.sum(jnp.where(idx[None] == eids[:, None, None], pstarts[:, None, None], 0),
                          axis=0)
    n_blocks = t * TOP_K // bm + N_EXPERTS
    n_rows = n_blocks * bm
    tm = min(t, 256)
    nt = t // tm
    assert nt >= 2
    dest_tiles = dest.reshape(TOP_K, nt, tm).transpose(1, 0, 2).reshape(nt, 1, TOP_K * tm)
    xs = _dispatch(pends, pcounts, dest_tiles, xn2, n_rows)
    blk_start = jnp.arange(n_blocks, dtype=I32) * bm
    blk_e = jnp.minimum(jnp.sum((pends[None, :] <= blk_start[:, None]).astype(I32), axis=1),
                        N_EXPERTS - 1)
    nused = (pends[-1] // bm).astype(I32).reshape(1)

    grp = 2 * LANES
    f2 = gu_w.shape[2]
    src = jnp.arange(grp, dtype=I32)[:, None]
    dst = jnp.arange(grp, dtype=I32)[None, :]
    perm = (src == jnp.where(dst < LANES, 2 * dst, 2 * (dst - LANES) + 1)).astype(BF16)
    gub = gu_b.reshape(N_EXPERTS, f2 // grp, LANES, 2).transpose(0, 1, 3, 2).reshape(N_EXPERTS, 1, f2)
    yb = _experts(blk_e, nused, xs, gu_w, gub, dn_w, dn_b.reshape(N_EXPERTS, 1, d), perm)

    return dest_tiles, yb, h, gates.T


def kernel(x, norm_mix_w, w_in, conv_w, mu_shift, w0, w_decay_up, a0, w_iclr_up, w_gate_up,
           k_k, k_a, r_k, ln_x_w, ln_x_b, w_out, norm_ffn_w, router_w, router_b,
           gu_w, gu_b, dn_w, dn_b, norm_final_w):
    bsz, seq, d = x.shape
    assert w_in.shape[0] == 1, "the final RMSNorm is fused into the single layer's combine step"
    dest_tiles, yb, h_mid, gates_t = _layer(
        x.reshape(bsz * seq, d), bsz, seq, norm_mix_w[0], w_in[0], conv_w[0], mu_shift[0], w0[0],
        w_decay_up[0], a0[0], w_iclr_up[0], w_gate_up[0], k_k[0], k_a[0], r_k[0], ln_x_w[0],
        ln_x_b[0], w_out[0], norm_ffn_w[0], router_w[0], router_b[0], gu_w[0], gu_b[0], dn_w[0],
        dn_b[0])
    out = _combine(dest_tiles, yb, h_mid, gates_t, norm_final_w.reshape(1, d))
    return out.reshape(bsz, seq, d)
```
